```python
import math
import jax, jax.numpy as jnp
from jax import lax
import numpy as np

D_MODEL = 1024
BATCH = 8
SEQ = 2048
DEPTH = 1
DEC_BATCH = 128
DEC_SEQ = 4
PAST_LEN = 16384
PAGE_SIZE = 128

D_FF = 2816
CONV_CH = 512
CONV_WIDTH = 31
GDN_HEADS = 4
GDN_DK = 128
GDN_DV = 128
QK_DIM = GDN_HEADS * GDN_DK
V_DIM = GDN_HEADS * GDN_DV
QKV_DIM = 2 * QK_DIM + V_DIM
SHORT_CONV_WIDTH = 4
GDN_CHUNK = 64
MIX_WIDTH = CONV_CH + V_DIM
PROJ_WIDTH = 2 * CONV_CH + QKV_DIM + V_DIM + 2 * GDN_HEADS
LN_EPS = 1e-5
RMS_EPS = 1e-6
DEEPNORM_ALPHA = (2.0 * DEPTH) ** 0.25
DEEPNORM_BETA = (8.0 * DEPTH) ** -0.25

kernel_name = "hymba_conformer_gdn_macaron_deepnorm_step"


def _layer_norm(x, g, b):
    x32 = x.astype(jnp.float32)
    mu = jnp.mean(x32, axis=-1, keepdims=True)
    var = jnp.mean(jnp.square(x32 - mu), axis=-1, keepdims=True)
    return ((x32 - mu) * lax.rsqrt(var + LN_EPS) * g.astype(jnp.float32) + b.astype(jnp.float32)).astype(x.dtype)


def _swiglu(x, w_up, w_down):
    a, b = jnp.split(x @ w_up, 2, axis=-1)
    return (jax.nn.silu(a) * b) @ w_down


def _causal_dwconv(x, buf, w):
    ch = x.shape[-1]
    width = w.shape[0]
    xp = jnp.concatenate([buf.astype(x.dtype), x], axis=1)
    y = lax.conv_general_dilated(xp, w.astype(x.dtype)[:, None, :], window_strides=(1,), padding='VALID',
                                 dimension_numbers=('NWC', 'WIO', 'NWC'), feature_group_count=ch)
    return y, xp[:, xp.shape[1] - (width - 1):]


def _l2norm(x):
    x32 = x.astype(jnp.float32)
    return x32 * lax.rsqrt(jnp.sum(jnp.square(x32), axis=-1, keepdims=True) + RMS_EPS)


def _gated_delta_chunked(q, k, v, g, beta, s0):
    f32 = jnp.float32
    B, T, H, DK = q.shape
    DV = v.shape[-1]
    C = min(GDN_CHUNK, T)
    pad = (-T) % C
    q = q.astype(f32) * (DK ** -0.5)
    k = k.astype(f32)
    v = v.astype(f32)
    g = g.astype(f32)
    beta = beta.astype(f32)
    if pad:
        pw = ((0, 0), (0, pad), (0, 0), (0, 0))
        q, k, v = jnp.pad(q, pw), jnp.pad(k, pw), jnp.pad(v, pw)
        g, beta = jnp.pad(g, pw[:3]), jnp.pad(beta, pw[:3])
    N = (T + pad) // C

    def chunks(a):
        return a.reshape(B, N, C, H, a.shape[-1]).transpose(0, 3, 1, 2, 4)

    qc, kc, vc = chunks(q), chunks(k), chunks(v)
    gc = g.reshape(B, N, C, H).transpose(0, 3, 1, 2)
    bc = beta.reshape(B, N, C, H).transpose(0, 3, 1, 2)
    gcum = jnp.cumsum(gc, axis=-1)
    idx = jnp.arange(C)
    causal = idx[:, None] >= idx[None, :]
    strict = idx[:, None] > idx[None, :]
    decay = jnp.exp(jnp.where(causal, gcum[..., :, None] - gcum[..., None, :], -jnp.inf))
    k_beta = kc * bc[..., None]
    v_beta = vc * bc[..., None]
    a_mat = jnp.where(strict, jnp.einsum('bhnid,bhnjd->bhnij', k_beta, kc) * decay, 0.0)
    m = a_mat + jnp.eye(C, dtype=f32)
    rhs = jnp.concatenate([v_beta, k_beta * jnp.exp(gcum)[..., None]], axis=-1)
    sol = lax.linalg.triangular_solve(m, rhs, left_side=True, lower=True, unit_diagonal=True)
    u, w = sol[..., :DV], sol[..., DV:]
    qk = jnp.einsum('bhnid,bhnjd->bhnij', qc, kc) * decay

    def step(s, xs):
        q_i, k_i, u_i, w_i, qk_i, g_i = xs
        v_new = u_i - jnp.einsum('bhck,bhkv->bhcv', w_i, s)
        o_i = (jnp.einsum('bhck,bhkv->bhcv', q_i * jnp.exp(g_i)[..., None], s)
               + jnp.einsum('bhij,bhjv->bhiv', qk_i, v_new))
        g_last = g_i[..., -1]
        s = (s * jnp.exp(g_last)[..., None, None]
             + jnp.einsum('bhck,bhcv->bhkv', k_i * jnp.exp(g_last[..., None] - g_i)[..., None], v_new))
        return s, o_i

    xs = (jnp.moveaxis(qc, 2, 0), jnp.moveaxis(kc, 2, 0), jnp.moveaxis(u, 2, 0),
          jnp.moveaxis(w, 2, 0), jnp.moveaxis(qk, 2, 0), jnp.moveaxis(gcum, 2, 0))
    s_final, o = lax.scan(step, s0.astype(f32), xs)
    o = o.transpose(1, 0, 3, 2, 4).reshape(B, N * C, H, DV)[:, :T]
    return o, s_final


def _layer(x, conv_buf, qkv_buf, s0, ffn1_w_up, ffn1_w_down, ln1_g, ln1_b, w_in, conv_w, conv_b,
           conv_ln_g, conv_ln_b, qkv_conv_w, a_log, dt_bias, gdn_norm_g, w_out, ln2_g, ln2_b,
           ffn2_w_up, ffn2_w_down, ln3_g, ln3_b):
    B, T, _ = x.shape
    f32 = jnp.float32
    x = _layer_norm(DEEPNORM_ALPHA * x + 0.5 * _swiglu(x, ffn1_w_up, ffn1_w_down), ln1_g, ln1_b)
    h = x @ w_in
    sizes = [CONV_CH, CONV_CH, QKV_DIM, V_DIM, GDN_HEADS, GDN_HEADS]
    cuts = [int(c) for c in np.cumsum(sizes)[:-1]]
    a_val, a_gate, qkv, z, b_logit, a_logit = jnp.split(h, cuts, axis=-1)
    u = a_val * jax.nn.sigmoid(a_gate)
    c, conv_buf_new = _causal_dwconv(u, conv_buf, conv_w)
    c = jax.nn.silu(_layer_norm(c + conv_b, conv_ln_g, conv_ln_b))
    qkv, qkv_buf_new = _causal_dwconv(qkv, qkv_buf, qkv_conv_w)
    qkv = jax.nn.silu(qkv)
    q, k, v = jnp.split(qkv, [QK_DIM, 2 * QK_DIM], axis=-1)
    q = _l2norm(q.reshape(B, T, GDN_HEADS, GDN_DK))
    k = _l2norm(k.reshape(B, T, GDN_HEADS, GDN_DK))
    v = v.reshape(B, T, GDN_HEADS, GDN_DV)
    beta = jax.nn.sigmoid(b_logit.astype(f32))
    g = -jnp.exp(a_log.astype(f32)) * jax.nn.softplus(a_logit.astype(f32) + dt_bias.astype(f32))
    o, s_new = _gated_delta_chunked(q, k, v, g, beta, s0)
    o = o * lax.rsqrt(jnp.mean(jnp.square(o), axis=-1, keepdims=True) + RMS_EPS) * gdn_norm_g.astype(f32)
    o = o * jax.nn.silu(z.astype(f32).reshape(B, T, GDN_HEADS, GDN_DV))
    o = o.reshape(B, T, V_DIM).astype(x.dtype)
    mix = jnp.concatenate([c, o], axis=-1) @ w_out
    x = _layer_norm(DEEPNORM_ALPHA * x + mix, ln2_g, ln2_b)
    x = _layer_norm(DEEPNORM_ALPHA * x + 0.5 * _swiglu(x, ffn2_w_up, ffn2_w_down), ln3_g, ln3_b)
    return x, conv_buf_new, qkv_buf_new, s_new.astype(s0.dtype)


def setup_inputs(seed: int = 0) -> dict:
    key = jax.random.key(seed)
    ks = jax.random.split(key, 32)
    L = DEPTH

    def nrm(k, shape, scale):
        return jax.random.normal(k, shape, jnp.float32) * scale

    dt = jnp.exp(jax.random.uniform(ks[20], (L, GDN_HEADS), jnp.float32, math.log(1e-3), math.log(1e-1)))
    return {
        "x_prompt": nrm(ks[0], (BATCH, SEQ, D_MODEL), 1.0),
        "x_sample": nrm(ks[1], (DEC_BATCH, DEC_SEQ, D_MODEL), 1.0),
        "state_conv": nrm(ks[2], (L, DEC_BATCH, CONV_WIDTH - 1, CONV_CH), 0.5),
        "state_qkv_conv": nrm(ks[3], (L, DEC_BATCH, SHORT_CONV_WIDTH - 1, QKV_DIM), 0.5),
        "state_recurrent": nrm(ks[4], (L, DEC_BATCH, GDN_HEADS, GDN_DK, GDN_DV), 0.3),
        "ffn1_w_up": nrm(ks[5], (L, D_MODEL, 2 * D_FF), D_MODEL ** -0.5),
        "ffn1_w_down": nrm(ks[6], (L, D_FF, D_MODEL), DEEPNORM_BETA * D_FF ** -0.5),
        "ln1_g": 1.0 + nrm(ks[7], (L, D_MODEL), 0.02),
        "ln1_b": nrm(ks[8], (L, D_MODEL), 0.02),
        "w_in": nrm(ks[9], (L, D_MODEL, PROJ_WIDTH), D_MODEL ** -0.5),
        "conv_w": nrm(ks[10], (L, CONV_WIDTH, CONV_CH), CONV_WIDTH ** -0.5),
        "conv_b": nrm(ks[11], (L, CONV_CH), 0.02),
        "conv_ln_g": 1.0 + nrm(ks[12], (L, CONV_CH), 0.02),
        "conv_ln_b": nrm(ks[13], (L, CONV_CH), 0.02),
        "qkv_conv_w": nrm(ks[14], (L, SHORT_CONV_WIDTH, QKV_DIM), SHORT_CONV_WIDTH ** -0.5),
        "a_log": jnp.log(jax.random.uniform(ks[15], (L, GDN_HEADS), jnp.float32, 1.0, 16.0)),
        "dt_bias": dt + jnp.log(-jnp.expm1(-dt)),
        "gdn_norm_g": 1.0 + nrm(ks[16], (L, GDN_DV), 0.02),
        "w_out": nrm(ks[17], (L, MIX_WIDTH, D_MODEL), DEEPNORM_BETA * MIX_WIDTH ** -0.5),
        "ln2_g": 1.0 + nrm(ks[18], (L, D_MODEL), 0.02),
        "ln2_b": nrm(ks[19], (L, D_MODEL), 0.02),
        "ffn2_w_up": nrm(ks[21], (L, D_MODEL, 2 * D_FF), D_MODEL ** -0.5),
        "ffn2_w_down": nrm(ks[22], (L, D_FF, D_MODEL), DEEPNORM_BETA * D_FF ** -0.5),
        "ln3_g": 1.0 + nrm(ks[23], (L, D_MODEL), 0.02),
        "ln3_b": nrm(ks[24], (L, D_MODEL), 0.02),
    }


def reference(x_prompt, x_sample, state_conv, state_qkv_conv, state_recurrent,
              ffn1_w_up, ffn1_w_down, ln1_g, ln1_b, w_in, conv_w, conv_b, conv_ln_g, conv_ln_b,
              qkv_conv_w, a_log, dt_bias, gdn_norm_g, w_out, ln2_g, ln2_b,
              ffn2_w_up, ffn2_w_down, ln3_g, ln3_b):
    dt_ = x_prompt.dtype
    yp, ys = x_prompt, x_sample
    conv_p, qkvc_p, rec_p, conv_s, qkvc_s, rec_s = [], [], [], [], [], []
    for l in range(DEPTH):
        weights = (ffn1_w_up[l], ffn1_w_down[l], ln1_g[l], ln1_b[l], w_in[l], conv_w[l], conv_b[l],
                   conv_ln_g[l], conv_ln_b[l], qkv_conv_w[l], a_log[l], dt_bias[l], gdn_norm_g[l],
                   w_out[l], ln2_g[l], ln2_b[l], ffn2_w_up[l], ffn2_w_down[l], ln3_g[l], ln3_b[l])
        zc = jnp.zeros((BATCH, CONV_WIDTH - 1, CONV_CH), dt_)
        zq = jnp.zeros((BATCH, SHORT_CONV_WIDTH - 1, QKV_DIM), dt_)
        zs = jnp.zeros((BATCH, GDN_HEADS, GDN_DK, GDN_DV), dt_)
        yp, c1, q1, s1 = _layer(yp, zc, zq, zs, *weights)
        ys, c2, q2, s2 = _layer(ys, state_conv[l], state_qkv_conv[l], state_recurrent[l], *weights)
        conv_p.append(c1); qkvc_p.append(q1); rec_p.append(s1)
        conv_s.append(c2); qkvc_s.append(q2); rec_s.append(s2)
    return (yp, ys,
            jnp.stack(conv_p), jnp.stack(qkvc_p), jnp.stack(rec_p),
            jnp.stack(conv_s), jnp.stack(qkvc_s), jnp.stack(rec_s))
```

```python
import functools

import jax
import jax.numpy as jnp
from jax import lax
from jax.experimental import pallas as pl
from jax.experimental.pallas import tpu as pltpu

F32 = jnp.float32
BF16 = jnp.bfloat16
HI = lax.Precision.HIGHEST

D_MODEL = 1024
D_FF = 2816
CONV_CH = 512
CONV_WIDTH = 31
HEADS = 4
DK = 128
DV = 128
QK_DIM = HEADS * DK
V_DIM = HEADS * DV
QKV_DIM = 2 * QK_DIM + V_DIM
SHORT_W = 4
CHUNK = 64
PROJ_MAIN = 2 * CONV_CH + QKV_DIM + V_DIM
LN_EPS = 1e-5
RMS_EPS = 1e-6

LANES = 128
SUBLANES = 8
FF_CHUNK = 256
CONV_HALO = 32
QKV_HALO = 8
CONV_ROWS = 64
VMEM_LIMIT = 56 * 1024 * 1024


def _dot(a, b, prec=None):
    return jnp.dot(a, b, preferred_element_type=F32, precision=prec)


def _dot_nt(a, b, prec=None):
    return lax.dot_general(a, b, (((1,), (1,)), ((), ())), preferred_element_type=F32, precision=prec)


def _dot_tn(a, b, prec=None):
    return lax.dot_general(a, b, (((0,), (0,)), ((), ())), preferred_element_type=F32, precision=prec)


def _bf(x):
    return x.astype(BF16)


def _layer_norm(x, g, b):
    mu = jnp.mean(x, axis=-1, keepdims=True)
    xc = x - mu
    var = jnp.mean(xc * xc, axis=-1, keepdims=True)
    return xc * lax.rsqrt(var + LN_EPS) * g + b


def _silu(x):
    return x * jax.nn.sigmoid(x)


def _l2norm(x):
    return x * lax.rsqrt(jnp.sum(x * x, axis=-1, keepdims=True) + RMS_EPS)


def _gate_params(sm, gp_ref):
    beta = jax.nn.sigmoid(sm)
    g = -jnp.exp(gp_ref[0:1, :]) * jax.nn.softplus(sm + gp_ref[1:2, :])
    return beta, g


def _gdn_out(o, z, norm_g):
    o = o * lax.rsqrt(jnp.mean(o * o, axis=-1, keepdims=True) + RMS_EPS) * norm_g
    return o * _silu(z)


def _ffn_ln_kernel(x_ref, wup_ref, wdn_ref, g_ref, b_ref, o_ref, xb_ref, acc_ref, *, alpha):
    xb_ref[...] = _bf(x_ref[...])
    for c in range(D_FF // FF_CHUNK):
        lo = c * FF_CHUNK
        a = _dot(xb_ref[...], wup_ref[:, lo:lo + FF_CHUNK])
        b = _dot(xb_ref[...], wup_ref[:, D_FF + lo:D_FF + lo + FF_CHUNK])
        h = _bf(_silu(a) * b)
        d = _dot(h, wdn_ref[lo:lo + FF_CHUNK, :])
        if c == 0:
            acc_ref[...] = d
        else:
            acc_ref[...] += d
    y = alpha * x_ref[...] + 0.5 * acc_ref[...]
    o_ref[...] = _layer_norm(y, g_ref[...], b_ref[...])


def _const_spec(shape):
    zeros = (0,) * len(shape)
    return pl.BlockSpec(shape, lambda *_: zeros, pipeline_mode=pl.Buffered(1))


def _ffn_ln(x, w_up, w_dn, g, b, *, alpha, tm):
    n = x.shape[0]
    assert n % tm == 0
    return pl.pallas_call(
        functools.partial(_ffn_ln_kernel, alpha=alpha),
        grid=(n // tm,),
        in_specs=[
            pl.BlockSpec((tm, D_MODEL), lambda i: (i, 0)),
            _const_spec((D_MODEL, 2 * D_FF)),
            _const_spec((D_FF, D_MODEL)),
            _const_spec((1, D_MODEL)),
            _const_spec((1, D_MODEL)),
        ],
        out_specs=pl.BlockSpec((tm, D_MODEL), lambda i: (i, 0)),
        out_shape=jax.ShapeDtypeStruct((n, D_MODEL), F32),
        scratch_shapes=[pltpu.VMEM((tm, D_MODEL), BF16), pltpu.VMEM((tm, D_MODEL), F32)],
        compiler_params=pltpu.CompilerParams(
            dimension_semantics=("parallel",), vmem_limit_bytes=VMEM_LIMIT),
        name="ffn_ln",
    )(x, w_up, w_dn, g, b)


def _chunk_delta_rule(q, k, v, beta, g, s):
    c = q.shape[0]
    row = lax.broadcasted_iota(jnp.int32, (c, c), 0)
    col = lax.broadcasted_iota(jnp.int32, (c, c), 1)
    causal = row >= col
    strict = row > col
    tri = jnp.where(causal, 1.0, 0.0).astype(F32)
    gcum = _dot(tri, g, HI)
    eg = jnp.exp(gcum)
    lane = lax.broadcasted_iota(jnp.int32, (c, LANES), 1)
    g1 = jnp.where(lane == 0, gcum, jnp.where(lane == 1, 1.0, 0.0))
    g2 = jnp.where(lane == 0, 1.0, jnp.where(lane == 1, -gcum, 0.0))
    diff = _dot_nt(g1, g2, HI)
    decay = jnp.exp(jnp.where(causal, diff, -jnp.inf))
    kb = k * beta
    kbf = _bf(k)
    a_mat = jnp.where(strict, _dot_nt(_bf(kb), kbf) * decay, 0.0)
    n_pow = -a_mat
    inv = jnp.where(row == col, 1.0, 0.0).astype(F32) + n_pow
    steps = max(c.bit_length() - 2, 0) if (c & (c - 1)) == 0 else c.bit_length() - 1
    for _ in range(steps):
        n_pow = _dot(n_pow, n_pow, HI)
        inv = inv + _dot(inv, n_pow, HI)
    rhs = jnp.concatenate([v * beta, kb * eg], axis=-1)
    sol = _dot(inv, rhs, HI)
    u, w = sol[:, :DV], sol[:, DV:]
    qk = _dot_nt(_bf(q), kbf) * decay
    sb = _bf(s)
    v_new = u - _dot(_bf(w), sb)
    o = _dot(_bf(q * eg), sb) + _dot(_bf(qk), _bf(v_new))
    g_last = gcum[c - 1:c, :]
    kd = k * jnp.exp(g_last - gcum)
    s_new = s * jnp.exp(g_last) + _dot_tn(_bf(kd), _bf(v_new))
    return o, s_new


def _prompt_mixer_kernel(x_ref, win_ref, wsm_ref, cw_ref, cb_ref, cg_ref, cbb_ref, qw_ref, gp_ref,
                         ng_ref, wout_ref, lg_ref, lb_ref,
                         y_ref, cst_ref, qst_ref, sst_ref,
                         hs_ref, uext_ref, qext_ref, qn_ref, kn_ref, vv_ref, bb_ref, gb_ref, mix_ref,
                         s_ref, *, alpha, tm):
    t = pl.program_id(1)

    @pl.when(t == 0)
    def _():
        uext_ref[0:CONV_HALO, :] = jnp.zeros((CONV_HALO, CONV_CH), F32)
        qext_ref[0:QKV_HALO, :] = jnp.zeros((QKV_HALO, QKV_DIM), F32)
        s_ref[...] = jnp.zeros_like(s_ref)

    x = x_ref[0]
    xb = _bf(x)
    hs_ref[...] = _dot(xb, win_ref[...])
    sm = _dot(xb, wsm_ref[...])

    uext_ref[CONV_HALO:CONV_HALO + tm, :] = hs_ref[:, 0:CONV_CH] * jax.nn.sigmoid(hs_ref[:, CONV_CH:2 * CONV_CH])
    off = CONV_HALO - (CONV_WIDTH - 1)
    for r0 in range(0, tm, CONV_ROWS):
        acc = cw_ref[0:1, :] * uext_ref[off + r0:off + r0 + CONV_ROWS, :]
        for j in range(1, CONV_WIDTH):
            acc = acc + cw_ref[j:j + 1, :] * uext_ref[off + r0 + j:off + r0 + j + CONV_ROWS, :]
        cc = _layer_norm(acc + cb_ref[...], cg_ref[...], cbb_ref[...])
        mix_ref[r0:r0 + CONV_ROWS, 0:CONV_CH] = _silu(cc)

    qext_ref[QKV_HALO:QKV_HALO + tm, :] = hs_ref[:, 2 * CONV_CH:2 * CONV_CH + QKV_DIM]
    qoff = QKV_HALO - (SHORT_W - 1)
    for r0 in range(0, tm, CONV_ROWS):
        acc = qw_ref[0:1, :] * qext_ref[qoff + r0:qoff + r0 + CONV_ROWS, :]
        for j in range(1, SHORT_W):
            acc = acc + qw_ref[j:j + 1, :] * qext_ref[qoff + r0 + j:qoff + r0 + j + CONV_ROWS, :]
        acc = _silu(acc)
        rows = slice(r0, r0 + CONV_ROWS)
        for h in range(HEADS):
            cs = slice(h * DK, (h + 1) * DK)
            qn_ref[rows, cs] = _l2norm(acc[:, h * DK:(h + 1) * DK]) * (DK ** -0.5)
            kn_ref[rows, cs] = _l2norm(acc[:, QK_DIM + h * DK:QK_DIM + (h + 1) * DK])
        vv_ref[rows, :] = acc[:, 2 * QK_DIM:]

    beta, g = _gate_params(sm, gp_ref)
    for h in range(HEADS):
        cs = slice(h * DK, (h + 1) * DK)
        bb_ref[:, cs] = jnp.broadcast_to(beta[:, h:h + 1], (tm, LANES))
        gb_ref[:, cs] = jnp.broadcast_to(g[:, HEADS + h:HEADS + h + 1], (tm, LANES))

    def chunk_body(c, carry):
        r0 = pl.multiple_of(c * CHUNK, CHUNK)
        rows = pl.ds(r0, CHUNK)
        for h in range(HEADS):
            cs = slice(h * DK, (h + 1) * DK)
            o, s_new = _chunk_delta_rule(qn_ref[rows, cs], kn_ref[rows, cs], vv_ref[rows, cs],
                                         bb_ref[rows, cs], gb_ref[rows, cs], s_ref[h])
            s_ref[h] = s_new
            zc = slice(2 * CONV_CH + QKV_DIM + h * DV, 2 * CONV_CH + QKV_DIM + (h + 1) * DV)
            mix_ref[rows, CONV_CH + h * DV:CONV_CH + (h + 1) * DV] = _gdn_out(o, hs_ref[rows, zc], ng_ref[...])
        return carry

    lax.fori_loop(0, tm // CHUNK, chunk_body, 0)

    y = alpha * x + _dot(_bf(mix_ref[...]), wout_ref[...])
    y_ref[0] = _layer_norm(y, lg_ref[...], lb_ref[...])

    uext_ref[0:CONV_HALO, :] = uext_ref[tm:tm + CONV_HALO, :]
    qext_ref[0:QKV_HALO, :] = qext_ref[tm:tm + QKV_HALO, :]

    @pl.when(t == pl.num_programs(1) - 1)
    def _():
        cst_ref[0] = uext_ref[off:CONV_HALO, :]
        qst_ref[0] = qext_ref[qoff:QKV_HALO, :]
        sst_ref[0] = s_ref[...]


def _prompt_mixer(x, win, wsm, cw, cb, cg, cbb, qw, gp, ng, wout, lg, lb, *, alpha, tm):
    bsz, seq, _ = x.shape
    assert seq % tm == 0 and tm % CHUNK == 0 and tm >= CONV_HALO
    consts = [win, wsm, cw, cb, cg, cbb, qw, gp, ng, wout, lg, lb]
    return pl.pallas_call(
        functools.partial(_prompt_mixer_kernel, alpha=alpha, tm=tm),
        grid=(bsz, seq // tm),
        in_specs=[pl.BlockSpec((1, tm, D_MODEL), lambda b, t: (b, t, 0))]
        + [_const_spec(a.shape) for a in consts],
        out_specs=[
            pl.BlockSpec((1, tm, D_MODEL), lambda b, t: (b, t, 0)),
            pl.BlockSpec((1, CONV_WIDTH - 1, CONV_CH), lambda b, t: (b, 0, 0)),
            pl.BlockSpec((1, SHORT_W - 1, QKV_DIM), lambda b, t: (b, 0, 0)),
            pl.BlockSpec((1, HEADS, DK, DV), lambda b, t: (b, 0, 0, 0)),
        ],
        out_shape=[
            jax.ShapeDtypeStruct((bsz, seq, D_MODEL), F32),
            jax.ShapeDtypeStruct((bsz, CONV_WIDTH - 1, CONV_CH), F32),
            jax.ShapeDtypeStruct((bsz, SHORT_W - 1, QKV_DIM), F32),
            jax.ShapeDtypeStruct((bsz, HEADS, DK, DV), F32),
        ],
        scratch_shapes=[
            pltpu.VMEM((tm, PROJ_MAIN), F32),
            pltpu.VMEM((CONV_HALO + tm, CONV_CH), F32),
            pltpu.VMEM((QKV_HALO + tm, QKV_DIM), F32),
            pltpu.VMEM((tm, QK_DIM), F32),
            pltpu.VMEM((tm, QK_DIM), F32),
            pltpu.VMEM((tm, V_DIM), F32),
            pltpu.VMEM((tm, QK_DIM), F32),
            pltpu.VMEM((tm, QK_DIM), F32),
            pltpu.VMEM((tm, CONV_CH + V_DIM), F32),
            pltpu.VMEM((HEADS, DK, DV), F32),
        ],
        compiler_params=pltpu.CompilerParams(
            dimension_semantics=("parallel", "arbitrary"), vmem_limit_bytes=VMEM_LIMIT),
        name="prompt_mixer",
    )(x, *consts)


def _sample_conv_kernel(x_ref, wa_ref, wg_ref, cst_ref, cw_ref, cpre_ref, cso_ref, *, nb, ts):
    xb = _bf(x_ref[...])
    glu_all = _dot(xb, wa_ref[...]) * jax.nn.sigmoid(_dot(xb, wg_ref[...]))
    glu = [glu_all[t * nb:(t + 1) * nb, :] for t in range(ts)]

    def conv_in(s):
        return cst_ref[s] if s < CONV_WIDTH - 1 else glu[s - (CONV_WIDTH - 1)]

    for t in range(ts):
        acc = cw_ref[0:1, :] * conv_in(t)
        for j in range(1, CONV_WIDTH):
            acc = acc + cw_ref[j:j + 1, :] * conv_in(t + j)
        cpre_ref[t * nb:(t + 1) * nb, :] = acc
    for s in range(CONV_WIDTH - 1):
        cso_ref[s] = conv_in(s + ts)


def _sample_conv(x, win, cst, cw, *, nb, ts):
    n_blk = CONV_CH // LANES
    return pl.pallas_call(
        functools.partial(_sample_conv_kernel, nb=nb, ts=ts),
        grid=(n_blk,),
        in_specs=[
            _const_spec(x.shape),
            pl.BlockSpec((D_MODEL, LANES), lambda j: (0, j)),
            pl.BlockSpec((D_MODEL, LANES), lambda j: (0, n_blk + j)),
            pl.BlockSpec((CONV_WIDTH - 1, nb, LANES), lambda j: (0, 0, j)),
            pl.BlockSpec((CONV_WIDTH, LANES), lambda j: (0, j)),
        ],
        out_specs=[
            pl.BlockSpec((ts * nb, LANES), lambda j: (0, j)),
            pl.BlockSpec((CONV_WIDTH - 1, nb, LANES), lambda j: (0, 0, j)),
        ],
        out_shape=[
            jax.ShapeDtypeStruct((ts * nb, CONV_CH), F32),
            jax.ShapeDtypeStruct(cst.shape, F32),
        ],
        compiler_params=pltpu.CompilerParams(
            dimension_semantics=("parallel",), vmem_limit_bytes=VMEM_LIMIT),
        name="sample_conv",
    )(x, win, win, cst, cw)


def _sample_mixer_kernel(x_ref, cpre_ref, qst_ref, s_ref, wqk_ref, wvz_ref, wsm_ref, cb_ref, cg_ref,
                         cbb_ref, qw_ref, gp_ref, ng_ref, wout_ref, lg_ref, lb_ref,
                         y_ref, qso_ref, so_ref,
                         hqk_ref, hvz_ref, mix_ref, lw_ref, u_ref, kd_ref, qk_ref, rs_ref,
                         *, alpha, nb, ts, bb):
    i = pl.program_id(0)
    n_slab = 2 * ts
    pairs = [(a, b) for a in range(ts) for b in range(a + 1)]

    def slab(t):
        return slice(t * nb, (t + 1) * nb)

    @pl.when(i == 0)
    def _():
        xb = _bf(x_ref[...])
        hqk_ref[...] = _dot(xb, wqk_ref[...])
        hvz_ref[...] = _dot(xb, wvz_ref[...])
        sm = _dot(xb, wsm_ref[...])

        cc = _layer_norm(cpre_ref[...] + cb_ref[...], cg_ref[...], cbb_ref[...])
        mix_ref[:, 0:CONV_CH] = _silu(cc)

        for s in range(SHORT_W - 1):
            t = s + ts - (SHORT_W - 1)
            qso_ref[s, :, 0:2 * QK_DIM] = hqk_ref[slab(t), :]
            qso_ref[s, :, 2 * QK_DIM:QKV_DIM] = hvz_ref[slab(t), 0:V_DIM]

        def short_conv(col, src_ref, src_col):
            outs = []
            for t in range(ts):
                acc = None
                for j in range(SHORT_W):
                    s = t + j
                    if s < SHORT_W - 1:
                        xin = qst_ref[s, :, col:col + LANES]
                    else:
                        xin = src_ref[slab(s - (SHORT_W - 1)), src_col:src_col + LANES]
                    term = qw_ref[j:j + 1, col:col + LANES] * xin
                    acc = term if acc is None else acc + term
                outs.append(_silu(acc))
            return outs

        beta_all, g_all = _gate_params(sm, gp_ref)
        zeros = jnp.zeros((nb, LANES), F32)
        for h in range(HEADS):
            cs = slice(h * DK, (h + 1) * DK)
            q = [_l2norm(a) * (DK ** -0.5) for a in short_conv(h * DK, hqk_ref, h * DK)]
            k = [_l2norm(a) for a in short_conv(QK_DIM + h * DK, hqk_ref, QK_DIM + h * DK)]
            v = short_conv(2 * QK_DIM + h * DV, hvz_ref, h * DV)
            beta = [jnp.broadcast_to(beta_all[slab(t), h:h + 1], (nb, LANES)) for t in range(ts)]
            gcum = []
            for t in range(ts):
                gt = jnp.broadcast_to(g_all[slab(t), HEADS + h:HEADS + h + 1], (nb, LANES))
                gcum.append(gt if t == 0 else gcum[-1] + gt)
            us, ws = [], []
            for a in range(ts):
                ua = v[a] * beta[a]
                wa = k[a] * beta[a] * jnp.exp(gcum[a])
                for b in range(a):
                    a_ab = beta[a] * jnp.sum(k[a] * k[b], axis=-1, keepdims=True) * jnp.exp(gcum[a] - gcum[b])
                    ua = ua - a_ab * us[b]
                    wa = wa - a_ab * ws[b]
                us.append(ua)
                ws.append(wa)
            for t in range(ts):
                lw_ref[h, slab(t), :] = ws[t]
                lw_ref[h, slab(ts + t), :] = q[t] * jnp.exp(gcum[t])
                u_ref[h, slab(t), :] = us[t]
                u_ref[h, slab(ts + t), :] = zeros
                kd_ref[h, slab(t), :] = k[t] * jnp.exp(gcum[ts - 1] - gcum[t])
                kd_ref[h, slab(ts + t), :] = jnp.exp(gcum[ts - 1]) if t == 0 else zeros
            for p, (a, b) in enumerate(pairs):
                qk_ref[slab(p), cs] = jnp.sum(q[a] * k[b], axis=-1, keepdims=True) * jnp.exp(gcum[a] - gcum[b])

    def seq_body(bl, carry):
        b = i * bb + bl
        rows = pl.ds(b, n_slab, stride=nb)
        keep = lax.broadcasted_iota(jnp.int32, (n_slab, LANES), 0) < ts
        for h in range(HEADS):
            cs = slice(h * DK, (h + 1) * DK)
            s = s_ref[bl, h]
            r = _dot(_bf(lw_ref[h, rows, :]), _bf(s))
            rs_ref[h, rows, :] = r
            v_new = jnp.where(keep, u_ref[h, rows, :] - r, 0.0)
            kd = kd_ref[h, rows, :]
            so_ref[bl, h] = s * kd[ts:ts + 1, :] + _dot_tn(_bf(kd), _bf(v_new))
        return carry

    lax.fori_loop(0, bb, seq_body, 0)

    @pl.when(i == pl.num_programs(0) - 1)
    def _():
        for h in range(HEADS):
            cs = slice(h * DK, (h + 1) * DK)
            v_new = [u_ref[h, slab(t), :] - rs_ref[h, slab(t), :] for t in range(ts)]
            for a in range(ts):
                o = rs_ref[h, slab(ts + a), :]
                for b in range(a + 1):
                    o = o + qk_ref[slab(pairs.index((a, b))), cs] * v_new[b]
                zc = slice(V_DIM + h * DV, V_DIM + (h + 1) * DV)
                mix_ref[slab(a), CONV_CH + h * DV:CONV_CH + (h + 1) * DV] = _gdn_out(
                    o, hvz_ref[slab(a), zc], ng_ref[...])
        y = alpha * x_ref[...] + _dot(_bf(mix_ref[...]), wout_ref[...])
        y_ref[...] = _layer_norm(y, lg_ref[...], lb_ref[...])


def _sample_mixer(x, cst, qst, s0, win, wsm, cw, cb, cg, cbb, qw, gp, ng, wout, lg, lb, *, alpha, bb):
    nb = s0.shape[0]
    ts = x.shape[0] // nb
    assert nb % bb == 0 and nb % SUBLANES == 0 and ts >= SHORT_W - 1
    cpre, cso = _sample_conv(x, win, cst, cw, nb=nb, ts=ts)
    consts = [wsm, cb, cg, cbb, qw, gp, ng, wout, lg, lb]
    n_pairs = ts * (ts + 1) // 2
    qk_blk = 2 * CONV_CH // (2 * QK_DIM)
    y, qso, so = pl.pallas_call(
        functools.partial(_sample_mixer_kernel, alpha=alpha, nb=nb, ts=ts, bb=bb),
        grid=(nb // bb,),
        in_specs=[
            _const_spec(x.shape), _const_spec(cpre.shape), _const_spec(qst.shape),
            pl.BlockSpec((bb, HEADS, DK, DV), lambda i: (i, 0, 0, 0)),
            pl.BlockSpec((D_MODEL, 2 * QK_DIM), lambda i: (0, qk_blk), pipeline_mode=pl.Buffered(1)),
            pl.BlockSpec((D_MODEL, 2 * V_DIM), lambda i: (0, qk_blk + 1), pipeline_mode=pl.Buffered(1)),
        ] + [_const_spec(a.shape) for a in consts],
        out_specs=[
            pl.BlockSpec(x.shape, lambda i: (0, 0)),
            pl.BlockSpec(qst.shape, lambda i: (0, 0, 0)),
            pl.BlockSpec((bb, HEADS, DK, DV), lambda i: (i, 0, 0, 0)),
        ],
        out_shape=[
            jax.ShapeDtypeStruct(x.shape, F32),
            jax.ShapeDtypeStruct(qst.shape, F32),
            jax.ShapeDtypeStruct(s0.shape, F32),
        ],
        scratch_shapes=[
            pltpu.VMEM((ts * nb, 2 * QK_DIM), F32),
            pltpu.VMEM((ts * nb, 2 * V_DIM), F32),
            pltpu.VMEM((ts * nb, CONV_CH + V_DIM), F32),
            pltpu.VMEM((HEADS, 2 * ts * nb, DK), F32),
            pltpu.VMEM((HEADS, 2 * ts * nb, DV), F32),
            pltpu.VMEM((HEADS, 2 * ts * nb, DK), F32),
            pltpu.VMEM((n_pairs * nb, QK_DIM), F32),
            pltpu.VMEM((HEADS, 2 * ts * nb, DV), F32),
        ],
        compiler_params=pltpu.CompilerParams(
            dimension_semantics=("arbitrary",), vmem_limit_bytes=VMEM_LIMIT),
        name="sample_mixer",
    )(x, cpre, qst, s0, win, win, *consts)
    return y, cso, qso, so


def _row(v):
    return v.reshape(1, -1).astype(F32)


def _layer_params(l, ffn1_w_up, ffn1_w_down, ln1_g, ln1_b, w_in, conv_w, conv_b, conv_ln_g, conv_ln_b,
                  qkv_conv_w, a_log, dt_bias, gdn_norm_g, w_out, ln2_g, ln2_b, ffn2_w_up, ffn2_w_down,
                  ln3_g, ln3_b):
    wsm = jnp.zeros((D_MODEL, LANES), BF16).at[:, 0:2 * HEADS].set(_bf(w_in[l][:, PROJ_MAIN:]))
    gp = jnp.zeros((2, LANES), F32)
    gp = gp.at[0, HEADS:2 * HEADS].set(a_log[l]).at[1, HEADS:2 * HEADS].set(dt_bias[l])
    ffn1 = (_bf(ffn1_w_up[l]), _bf(ffn1_w_down[l]), _row(ln1_g[l]), _row(ln1_b[l]))
    ffn2 = (_bf(ffn2_w_up[l]), _bf(ffn2_w_down[l]), _row(ln3_g[l]), _row(ln3_b[l]))
    mixer = (_bf(w_in[l][:, :PROJ_MAIN]), wsm, conv_w[l], _row(conv_b[l]), _row(conv_ln_g[l]),
             _row(conv_ln_b[l]), qkv_conv_w[l], gp, _row(gdn_norm_g[l]), _bf(w_out[l]),
             _row(ln2_g[l]), _row(ln2_b[l]))
    return ffn1, mixer, ffn2


def kernel(x_prompt, x_sample, state_conv, state_qkv_conv, state_recurrent, ffn1_w_up, ffn1_w_down, ln1_g, ln1_b, w_in, conv_w, conv_b, conv_ln_g, conv_ln_b, qkv_conv_w, a_log, dt_bias, gdn_norm_g, w_out, ln2_g, ln2_b, ffn2_w_up, ffn2_w_down, ln3_g, ln3_b):
    depth = w_in.shape[0]
    alpha = (2.0 * depth) ** 0.25
    bsz, seq, _ = x_prompt.shape
    nb, ts, _ = x_sample.shape
    tm_p = min(512, seq)
    yp = x_prompt
    ys = jnp.swapaxes(x_sample, 0, 1).reshape(ts * nb, D_MODEL)
    outs = [[] for _ in range(6)]
    for l in range(depth):
        ffn1, mixer, ffn2 = _layer_params(
            l, ffn1_w_up, ffn1_w_down, ln1_g, ln1_b, w_in, conv_w, conv_b, conv_ln_g, conv_ln_b,
            qkv_conv_w, a_log, dt_bias, gdn_norm_g, w_out, ln2_g, ln2_b, ffn2_w_up, ffn2_w_down,
            ln3_g, ln3_b)
        x1 = _ffn_ln(yp.reshape(bsz * seq, D_MODEL), *ffn1, alpha=alpha, tm=tm_p)
        x2, c1, q1, s1 = _prompt_mixer(x1.reshape(bsz, seq, D_MODEL), *mixer, alpha=alpha, tm=tm_p)
        yp = _ffn_ln(x2.reshape(bsz * seq, D_MODEL), *ffn2, alpha=alpha, tm=tm_p).reshape(bsz, seq, D_MODEL)
        z1 = _ffn_ln(ys, *ffn1, alpha=alpha, tm=ts * nb)
        z2, c2, q2, s2 = _sample_mixer(
            z1, jnp.swapaxes(state_conv[l], 0, 1), jnp.swapaxes(state_qkv_conv[l], 0, 1),
            state_recurrent[l], *mixer, alpha=alpha, bb=min(8, nb))
        ys = _ffn_ln(z2, *ffn2, alpha=alpha, tm=ts * nb)
        for lst, val in zip(outs, (c1, q1, s1, jnp.swapaxes(c2, 0, 1), jnp.swapaxes(q2, 0, 1), s2)):
            lst.append(val)
    ys = jnp.swapaxes(ys.reshape(ts, nb, D_MODEL), 0, 1)
    return (yp, ys) + tuple(jnp.stack(o) for o in outs)
```

```python
import functools

import jax
import jax.numpy as jnp
from jax import lax
from jax.experimental import pallas as pl
from jax.experimental.pallas import tpu as pltpu

F32 = jnp.float32
BF16 = jnp.bfloat16
HI = lax.Precision.HIGHEST

D_MODEL = 1024
D_FF = 2816
CONV_CH = 512
CONV_WIDTH = 31
HEADS = 4
DK = 128
DV = 128
QK_DIM = HEADS * DK
V_DIM = HEADS * DV
QKV_DIM = 2 * QK_DIM + V_DIM
SHORT_W = 4
CHUNK = 64
PROJ_MAIN = 2 * CONV_CH + QKV_DIM + V_DIM
LN_EPS = 1e-5
RMS_EPS = 1e-6

LANES = 128
SUBLANES = 8
FF_CHUNK = 256
CONV_HALO = 32
QKV_HALO = 8
CONV_ROWS = 64
INV_PREC = None
PREPARE_GROUP = 4
VMEM_LIMIT = 56 * 1024 * 1024


def _dot(a, b, prec=None):
    return jnp.dot(a, b, preferred_element_type=F32, precision=prec)


def _dot_nt(a, b, prec=None):
    return lax.dot_general(a, b, (((1,), (1,)), ((), ())), preferred_element_type=F32, precision=prec)


def _dot_tn(a, b, prec=None):
    return lax.dot_general(a, b, (((0,), (0,)), ((), ())), preferred_element_type=F32, precision=prec)


def _bf(x):
    return x.astype(BF16)


def _layer_norm(x, g, b):
    mu = jnp.mean(x, axis=-1, keepdims=True)
    xc = x - mu
    var = jnp.mean(xc * xc, axis=-1, keepdims=True)
    return xc * lax.rsqrt(var + LN_EPS) * g + b


def _silu(x):
    return x * jax.nn.sigmoid(x)


def _l2norm(x):
    return x * lax.rsqrt(jnp.sum(x * x, axis=-1, keepdims=True) + RMS_EPS)


def _gate_params(sm, gp_ref):
    beta = jax.nn.sigmoid(sm)
    g = -jnp.exp(gp_ref[0:1, :]) * jax.nn.softplus(sm + gp_ref[1:2, :])
    return beta, g


def _gdn_out(o, z, norm_g):
    o = o * lax.rsqrt(jnp.mean(o * o, axis=-1, keepdims=True) + RMS_EPS) * norm_g
    return o * _silu(z)


def _ffn_ln_kernel(x_ref, wup_ref, wdn_ref, g_ref, b_ref, o_ref, xb_ref, acc_ref, *, alpha):
    xb_ref[...] = _bf(x_ref[...])
    for c in range(D_FF // FF_CHUNK):
        lo = c * FF_CHUNK
        a = _dot(xb_ref[...], wup_ref[:, lo:lo + FF_CHUNK])
        b = _dot(xb_ref[...], wup_ref[:, D_FF + lo:D_FF + lo + FF_CHUNK])
        h = _bf(_silu(a) * b)
        d = _dot(h, wdn_ref[lo:lo + FF_CHUNK, :])
        if c == 0:
            acc_ref[...] = d
        else:
            acc_ref[...] += d
    y = alpha * x_ref[...] + 0.5 * acc_ref[...]
    o_ref[...] = _layer_norm(y, g_ref[...], b_ref[...])


def _const_spec(shape):
    zeros = (0,) * len(shape)
    return pl.BlockSpec(shape, lambda *_: zeros, pipeline_mode=pl.Buffered(1))


def _ffn_ln(x, w_up, w_dn, g, b, *, alpha, tm):
    n = x.shape[0]
    assert n % tm == 0
    return pl.pallas_call(
        functools.partial(_ffn_ln_kernel, alpha=alpha),
        grid=(n // tm,),
        in_specs=[
            pl.BlockSpec((tm, D_MODEL), lambda i: (i, 0)),
            _const_spec((D_MODEL, 2 * D_FF)),
            _const_spec((D_FF, D_MODEL)),
            _const_spec((1, D_MODEL)),
            _const_spec((1, D_MODEL)),
        ],
        out_specs=pl.BlockSpec((tm, D_MODEL), lambda i: (i, 0)),
        out_shape=jax.ShapeDtypeStruct((n, D_MODEL), F32),
        scratch_shapes=[pltpu.VMEM((tm, D_MODEL), BF16), pltpu.VMEM((tm, D_MODEL), F32)],
        compiler_params=pltpu.CompilerParams(
            dimension_semantics=("parallel",), vmem_limit_bytes=VMEM_LIMIT),
        name="ffn_ln",
    )(x, w_up, w_dn, g, b)


def _mm(a, b, prec):
    if prec is None:
        return _dot(_bf(a), _bf(b))
    return _dot(a, b, prec)


def _delta_prepare(group, qn_ref, kn_ref, vv_ref, bn_ref, gn_ref, wq_ref, u_ref, kd_ref, qk_ref, el_ref):
    c = CHUNK
    row = lax.broadcasted_iota(jnp.int32, (c, LANES), 0)
    lane = lax.broadcasted_iota(jnp.int32, (c, LANES), 1)
    low = lane < c
    causal = row >= lane
    strict = row > lane
    eye_hi = jnp.where(lane == row + c, 1.0, 0.0).astype(F32)
    tri = jnp.where(causal[:, :c], 1.0, 0.0).astype(F32)
    tri_t = jnp.where((lane >= row) & low, 1.0, 0.0).astype(F32)
    zeros = jnp.zeros((c, LANES), F32)
    pad = jnp.zeros((c, 2 * LANES), F32)

    chunks = range(PREPARE_GROUP)
    base = group * PREPARE_GROUP
    rows = {ch: pl.ds(pl.multiple_of((base + ch) * c, c), c) for ch in chunks}
    rows2 = {ch: pl.ds(pl.multiple_of((base + ch) * 2 * c, 2 * c), 2 * c) for ch in chunks}
    tile8 = {ch: pl.ds(pl.multiple_of((base + ch) * SUBLANES, SUBLANES), SUBLANES) for ch in chunks}
    g_n ={ch: gn_ref[rows[ch], :] for ch in chunks}
    beta_n = {ch: bn_ref[rows[ch], :] for ch in chunks}
    gcum_n = {ch: _dot(tri, g_n[ch], HI) for ch in chunks}
    gcum_t = {ch: _dot_tn(g_n[ch], tri_t, HI) for ch in chunks}

    items = [(ch, h) for ch in chunks for h in range(HEADS)]
    cs = {it: slice(it[1] * DK, (it[1] + 1) * DK) for it in items}
    gcol = {(ch, h): jnp.broadcast_to(gcum_n[ch][:, HEADS + h:HEADS + h + 1], (c, LANES)) for ch, h in items}
    beta = {(ch, h): jnp.broadcast_to(beta_n[ch][:, h:h + 1], (c, LANES)) for ch, h in items}
    decay = {(ch, h): jnp.exp(jnp.where(causal, gcol[ch, h] - gcum_t[ch][HEADS + h:HEADS + h + 1, :], -jnp.inf))
             for ch, h in items}
    k = {it: kn_ref[rows[it[0]], cs[it]] for it in items}
    q = {it: qn_ref[rows[it[0]], cs[it]] for it in items}
    kb = {it: k[it] * beta[it] for it in items}
    kq = {it: _dot_nt(_bf(jnp.concatenate([kb[it], q[it]], axis=0)), _bf(jnp.concatenate([k[it], zeros], axis=0)))
          for it in items}
    z = {it: jnp.where(low, jnp.where(strict, -kq[it][:c] * decay[it], 0.0), eye_hi) for it in items}
    for _ in range(c.bit_length() - 1):
        z = {it: _mm(z[it][:, :c], z[it], INV_PREC) + jnp.where(low, 0.0, z[it]) for it in items}
    eg = {it: jnp.exp(gcol[it]) for it in items}
    rhs = {it: jnp.concatenate([vv_ref[rows[it[0]], cs[it]] * beta[it], kb[it] * eg[it]], axis=1) for it in items}
    sol = {it: _mm(z[it], jnp.concatenate([pad, rhs[it]], axis=0), INV_PREC) for it in items}
    for it in items:
        ch = it[0]
        g_last = gcol[it][c - 1:c, :]
        wq_ref[rows2[ch], cs[it]] = jnp.concatenate([sol[it][:, DV:], q[it] * eg[it]], axis=0)
        u_ref[rows[ch], cs[it]] = sol[it][:, :DV]
        kd_ref[rows[ch], cs[it]] = k[it] * jnp.exp(g_last - gcol[it])
        qk_ref[rows[ch], cs[it]] = kq[it][c:] * decay[it]
        el_ref[tile8[ch], cs[it]] = jnp.broadcast_to(jnp.exp(g_last), (SUBLANES, LANES))


def _delta_scan_chunk(rows, rows2, tile8, wq_ref, u_ref, kd_ref, qk_ref, el_ref, s_ref):
    c = CHUNK
    heads = range(HEADS)
    cs = [slice(h * DK, (h + 1) * DK) for h in heads]
    s = [s_ref[h] for h in heads]
    r = [_dot(_bf(wq_ref[rows2, cs[h]]), _bf(s[h])) for h in heads]
    v_new = [u_ref[rows, cs[h]] - r[h][:c] for h in heads]
    o = [r[h][c:] + _dot(_bf(qk_ref[rows, cs[h]][:, :c]), _bf(v_new[h])) for h in heads]
    for h in heads:
        s_ref[h] = s[h] * el_ref[tile8, cs[h]][0:1, :] + _dot_tn(_bf(kd_ref[rows, cs[h]]), _bf(v_new[h]))
    return o


def _prompt_mixer_kernel(x_ref, win_ref, wsm_ref, cw_ref, cb_ref, cg_ref, cbb_ref, qw_ref, gp_ref,
                         ng_ref, wout_ref, lg_ref, lb_ref,
                         y_ref, cst_ref, qst_ref, sst_ref,
                         hs_ref, uext_ref, qext_ref, qn_ref, kn_ref, vv_ref, bn_ref, gn_ref, mix_ref,
                         wq_ref, u_ref, kd_ref, qk_ref, el_ref, s_ref, *, alpha, tm):
    t = pl.program_id(1)

    @pl.when(t == 0)
    def _():
        uext_ref[0:CONV_HALO, :] = jnp.zeros((CONV_HALO, CONV_CH), F32)
        qext_ref[0:QKV_HALO, :] = jnp.zeros((QKV_HALO, QKV_DIM), F32)
        s_ref[...] = jnp.zeros_like(s_ref)

    xb = _bf(x_ref[0])
    hs_ref[...] = _dot(xb, win_ref[...])
    sm = _dot(xb, wsm_ref[...])

    uext_ref[CONV_HALO:CONV_HALO + tm, :] = hs_ref[:, 0:CONV_CH] * jax.nn.sigmoid(hs_ref[:, CONV_CH:2 * CONV_CH])
    off = CONV_HALO - (CONV_WIDTH - 1)
    for r0 in range(0, tm, CONV_ROWS):
        acc = cw_ref[0:1, :] * uext_ref[off + r0:off + r0 + CONV_ROWS, :]
        for j in range(1, CONV_WIDTH):
            acc = acc + cw_ref[j:j + 1, :] * uext_ref[off + r0 + j:off + r0 + j + CONV_ROWS, :]
        cc = _layer_norm(acc + cb_ref[...], cg_ref[...], cbb_ref[...])
        mix_ref[r0:r0 + CONV_ROWS, 0:CONV_CH] = _silu(cc)

    qext_ref[QKV_HALO:QKV_HALO + tm, :] = hs_ref[:, 2 * CONV_CH:2 * CONV_CH + QKV_DIM]
    qoff = QKV_HALO - (SHORT_W - 1)
    for r0 in range(0, tm, CONV_ROWS):
        acc = qw_ref[0:1, :] * qext_ref[qoff + r0:qoff + r0 + CONV_ROWS, :]
        for j in range(1, SHORT_W):
            acc = acc + qw_ref[j:j + 1, :] * qext_ref[qoff + r0 + j:qoff + r0 + j + CONV_ROWS, :]
        acc = _silu(acc)
        rows = slice(r0, r0 + CONV_ROWS)
        for h in range(HEADS):
            cs = slice(h * DK, (h + 1) * DK)
            qn_ref[rows, cs] = _l2norm(acc[:, h * DK:(h + 1) * DK]) * (DK ** -0.5)
            kn_ref[rows, cs] = _l2norm(acc[:, QK_DIM + h * DK:QK_DIM + (h + 1) * DK])
        vv_ref[rows, :] = acc[:, 2 * QK_DIM:]

    bn_ref[...], gn_ref[...] = _gate_params(sm, gp_ref)

    def chunk_slices(c):
        rows = pl.ds(pl.multiple_of(c * CHUNK, CHUNK), CHUNK)
        rows2 = pl.ds(pl.multiple_of(c * 2 * CHUNK, 2 * CHUNK), 2 * CHUNK)
        tile8 = pl.ds(pl.multiple_of(c * SUBLANES, SUBLANES), SUBLANES)
        return rows, rows2, tile8

    def prepare_body(group, carry):
        _delta_prepare(group, qn_ref, kn_ref, vv_ref, bn_ref, gn_ref, wq_ref, u_ref, kd_ref, qk_ref, el_ref)
        return carry

    lax.fori_loop(0, tm // (CHUNK * PREPARE_GROUP), prepare_body, 0)

    def scan_body(c, carry):
        rows, rows2, tile8 = chunk_slices(c)
        o = _delta_scan_chunk(rows, rows2, tile8, wq_ref, u_ref, kd_ref, qk_ref, el_ref, s_ref)
        for h in range(HEADS):
            zc = slice(2 * CONV_CH + QKV_DIM + h * DV, 2 * CONV_CH + QKV_DIM + (h + 1) * DV)
            mix_ref[rows, CONV_CH + h * DV:CONV_CH + (h + 1) * DV] = _gdn_out(o[h], hs_ref[rows, zc], ng_ref[...])
        return carry

    lax.fori_loop(0, tm // CHUNK, scan_body, 0)

    y = alpha * x_ref[0] + _dot(_bf(mix_ref[...]), wout_ref[...])
    y_ref[0] = _layer_norm(y, lg_ref[...], lb_ref[...])

    uext_ref[0:CONV_HALO, :] = uext_ref[tm:tm + CONV_HALO, :]
    qext_ref[0:QKV_HALO, :] = qext_ref[tm:tm + QKV_HALO, :]

    @pl.when(t == pl.num_programs(1) - 1)
    def _():
        cst_ref[0] = uext_ref[off:CONV_HALO, :]
        qst_ref[0] = qext_ref[qoff:QKV_HALO, :]
        sst_ref[0] = s_ref[...]


def _prompt_mixer(x, win, wsm, cw, cb, cg, cbb, qw, gp, ng, wout, lg, lb, *, alpha, tm):
    bsz, seq, _ = x.shape
    assert seq % tm == 0 and tm % (CHUNK * PREPARE_GROUP) == 0 and tm >= CONV_HALO
    consts = [win, wsm, cw, cb, cg, cbb, qw, gp, ng, wout, lg, lb]
    return pl.pallas_call(
        functools.partial(_prompt_mixer_kernel, alpha=alpha, tm=tm),
        grid=(bsz, seq // tm),
        in_specs=[pl.BlockSpec((1, tm, D_MODEL), lambda b, t: (b, t, 0))]
        + [_const_spec(a.shape) for a in consts],
        out_specs=[
            pl.BlockSpec((1, tm, D_MODEL), lambda b, t: (b, t, 0)),
            pl.BlockSpec((1, CONV_WIDTH - 1, CONV_CH), lambda b, t: (b, 0, 0)),
            pl.BlockSpec((1, SHORT_W - 1, QKV_DIM), lambda b, t: (b, 0, 0)),
            pl.BlockSpec((1, HEADS, DK, DV), lambda b, t: (b, 0, 0, 0)),
        ],
        out_shape=[
            jax.ShapeDtypeStruct((bsz, seq, D_MODEL), F32),
            jax.ShapeDtypeStruct((bsz, CONV_WIDTH - 1, CONV_CH), F32),
            jax.ShapeDtypeStruct((bsz, SHORT_W - 1, QKV_DIM), F32),
            jax.ShapeDtypeStruct((bsz, HEADS, DK, DV), F32),
        ],
        scratch_shapes=[
            pltpu.VMEM((tm, PROJ_MAIN), F32),
            pltpu.VMEM((CONV_HALO + tm, CONV_CH), F32),
            pltpu.VMEM((QKV_HALO + tm, QKV_DIM), F32),
            pltpu.VMEM((tm, QK_DIM), F32),
            pltpu.VMEM((tm, QK_DIM), F32),
            pltpu.VMEM((tm, V_DIM), F32),
            pltpu.VMEM((tm, LANES), F32),
            pltpu.VMEM((tm, LANES), F32),
            pltpu.VMEM((tm, CONV_CH + V_DIM), F32),
            pltpu.VMEM((2 * tm, QK_DIM), F32),
            pltpu.VMEM((tm, V_DIM), F32),
            pltpu.VMEM((tm, QK_DIM), F32),
            pltpu.VMEM((tm, QK_DIM), F32),
            pltpu.VMEM((tm // CHUNK * SUBLANES, QK_DIM), F32),
            pltpu.VMEM((HEADS, DK, DV), F32),
        ],
        compiler_params=pltpu.CompilerParams(
            dimension_semantics=("parallel", "arbitrary"), vmem_limit_bytes=VMEM_LIMIT),
        name="prompt_mixer",
    )(x, *consts)


def _sample_conv_kernel(x_ref, wa_ref, wg_ref, cst_ref, cw_ref, cpre_ref, cso_ref, *, nb, ts):
    xb = _bf(x_ref[...])
    glu_all = _dot(xb, wa_ref[...]) * jax.nn.sigmoid(_dot(xb, wg_ref[...]))
    glu = [glu_all[t * nb:(t + 1) * nb, :] for t in range(ts)]

    def conv_in(s):
        return cst_ref[s] if s < CONV_WIDTH - 1 else glu[s - (CONV_WIDTH - 1)]

    for t in range(ts):
        acc = cw_ref[0:1, :] * conv_in(t)
        for j in range(1, CONV_WIDTH):
            acc = acc + cw_ref[j:j + 1, :] * conv_in(t + j)
        cpre_ref[t * nb:(t + 1) * nb, :] = acc
    for s in range(CONV_WIDTH - 1):
        cso_ref[s] = conv_in(s + ts)


def _sample_conv(x, win, cst, cw, *, nb, ts):
    n_blk = CONV_CH // LANES
    return pl.pallas_call(
        functools.partial(_sample_conv_kernel, nb=nb, ts=ts),
        grid=(n_blk,),
        in_specs=[
            _const_spec(x.shape),
            pl.BlockSpec((D_MODEL, LANES), lambda j: (0, j)),
            pl.BlockSpec((D_MODEL, LANES), lambda j: (0, n_blk + j)),
            pl.BlockSpec((CONV_WIDTH - 1, nb, LANES), lambda j: (0, 0, j)),
            pl.BlockSpec((CONV_WIDTH, LANES), lambda j: (0, j)),
        ],
        out_specs=[
            pl.BlockSpec((ts * nb, LANES), lambda j: (0, j)),
            pl.BlockSpec((CONV_WIDTH - 1, nb, LANES), lambda j: (0, 0, j)),
        ],
        out_shape=[
            jax.ShapeDtypeStruct((ts * nb, CONV_CH), F32),
            jax.ShapeDtypeStruct(cst.shape, F32),
        ],
        compiler_params=pltpu.CompilerParams(
            dimension_semantics=("parallel",), vmem_limit_bytes=VMEM_LIMIT),
        name="sample_conv",
    )(x, win, win, cst, cw)


def _sample_mixer_kernel(x_ref, cpre_ref, qst_ref, s_ref, wqk_ref, wvz_ref, wsm_ref, cb_ref, cg_ref,
                         cbb_ref, qw_ref, gp_ref, ng_ref, wout_ref, lg_ref, lb_ref,
                         y_ref, qso_ref, so_ref,
                         hqk_ref, hvz_ref, mix_ref, lw_ref, u_ref, kd_ref, qk_ref, rs_ref,
                         *, alpha, nb, ts, bb):
    i = pl.program_id(0)
    n_slab = 2 * ts
    pairs = [(a, b) for a in range(ts) for b in range(a + 1)]

    def slab(t):
        return slice(t * nb, (t + 1) * nb)

    @pl.when(i == 0)
    def _():
        xb = _bf(x_ref[...])
        hqk_ref[...] = _dot(xb, wqk_ref[...])
        hvz_ref[...] = _dot(xb, wvz_ref[...])
        sm = _dot(xb, wsm_ref[...])

        cc = _layer_norm(cpre_ref[...] + cb_ref[...], cg_ref[...], cbb_ref[...])
        mix_ref[:, 0:CONV_CH] = _silu(cc)

        for s in range(SHORT_W - 1):
            t = s + ts - (SHORT_W - 1)
            qso_ref[s, :, 0:2 * QK_DIM] = hqk_ref[slab(t), :]
            qso_ref[s, :, 2 * QK_DIM:QKV_DIM] = hvz_ref[slab(t), 0:V_DIM]

        def short_conv(col, src_ref, src_col):
            outs = []
            for t in range(ts):
                acc = None
                for j in range(SHORT_W):
                    s = t + j
                    if s < SHORT_W - 1:
                        xin = qst_ref[s, :, col:col + LANES]
                    else:
                        xin = src_ref[slab(s - (SHORT_W - 1)), src_col:src_col + LANES]
                    term = qw_ref[j:j + 1, col:col + LANES] * xin
                    acc = term if acc is None else acc + term
                outs.append(_silu(acc))
            return outs

        beta_all, g_all = _gate_params(sm, gp_ref)
        zeros = jnp.zeros((nb, LANES), F32)
        for h in range(HEADS):
            cs = slice(h * DK, (h + 1) * DK)
            q = [_l2norm(a) * (DK ** -0.5) for a in short_conv(h * DK, hqk_ref, h * DK)]
            k = [_l2norm(a) for a in short_conv(QK_DIM + h * DK, hqk_ref, QK_DIM + h * DK)]
            v = short_conv(2 * QK_DIM + h * DV, hvz_ref, h * DV)
            beta = [jnp.broadcast_to(beta_all[slab(t), h:h + 1], (nb, LANES)) for t in range(ts)]
            gcum = []
            for t in range(ts):
                gt = jnp.broadcast_to(g_all[slab(t), HEADS + h:HEADS + h + 1], (nb, LANES))
                gcum.append(gt if t == 0 else gcum[-1] + gt)
            us, ws = [], []
            for a in range(ts):
                ua = v[a] * beta[a]
                wa = k[a] * beta[a] * jnp.exp(gcum[a])
                for b in range(a):
                    a_ab = beta[a] * jnp.sum(k[a] * k[b], axis=-1, keepdims=True) * jnp.exp(gcum[a] - gcum[b])
                    ua = ua - a_ab * us[b]
                    wa = wa - a_ab * ws[b]
                us.append(ua)
                ws.append(wa)
            for t in range(ts):
                lw_ref[h, slab(t), :] = ws[t]
                lw_ref[h, slab(ts + t), :] = q[t] * jnp.exp(gcum[t])
                u_ref[h, slab(t), :] = us[t]
                u_ref[h, slab(ts + t), :] = zeros
                kd_ref[h, slab(t), :] = k[t] * jnp.exp(gcum[ts - 1] - gcum[t])
                kd_ref[h, slab(ts + t), :] = jnp.exp(gcum[ts - 1]) if t == 0 else zeros
            for p, (a, b) in enumerate(pairs):
                qk_ref[slab(p), cs] = jnp.sum(q[a] * k[b], axis=-1, keepdims=True) * jnp.exp(gcum[a] - gcum[b])

    def seq_body(bl, carry):
        b = i * bb + bl
        rows = pl.ds(b, n_slab, stride=nb)
        keep = lax.broadcasted_iota(jnp.int32, (n_slab, LANES), 0) < ts
        for h in range(HEADS):
            cs = slice(h * DK, (h + 1) * DK)
            s = s_ref[bl, h]
            r = _dot(_bf(lw_ref[h, rows, :]), _bf(s))
            rs_ref[h, rows, :] = r
            v_new = jnp.where(keep, u_ref[h, rows, :] - r, 0.0)
            kd = kd_ref[h, rows, :]
            so_ref[bl, h] = s * kd[ts:ts + 1, :] + _dot_tn(_bf(kd), _bf(v_new))
        return carry

    lax.fori_loop(0, bb, seq_body, 0)

    @pl.when(i == pl.num_programs(0) - 1)
    def _():
        for h in range(HEADS):
            cs = slice(h * DK, (h + 1) * DK)
            v_new = [u_ref[h, slab(t), :] - rs_ref[h, slab(t), :] for t in range(ts)]
            for a in range(ts):
                o = rs_ref[h, slab(ts + a), :]
                for b in range(a + 1):
                    o = o + qk_ref[slab(pairs.index((a, b))), cs] * v_new[b]
                zc = slice(V_DIM + h * DV, V_DIM + (h + 1) * DV)
                mix_ref[slab(a), CONV_CH + h * DV:CONV_CH + (h + 1) * DV] = _gdn_out(
                    o, hvz_ref[slab(a), zc], ng_ref[...])
        y = alpha * x_ref[...] + _dot(_bf(mix_ref[...]), wout_ref[...])
        y_ref[...] = _layer_norm(y, lg_ref[...], lb_ref[...])


def _sample_mixer(x, cst, qst, s0, win, wsm, cw, cb, cg, cbb, qw, gp, ng, wout, lg, lb, *, alpha, bb):
    nb = s0.shape[0]
    ts = x.shape[0] // nb
    assert nb % bb == 0 and nb % SUBLANES == 0 and ts >= SHORT_W - 1
    cpre, cso = _sample_conv(x, win, cst, cw, nb=nb, ts=ts)
    consts = [wsm, cb, cg, cbb, qw, gp, ng, wout, lg, lb]
    n_pairs = ts * (ts + 1) // 2
    qk_blk = 2 * CONV_CH // (2 * QK_DIM)
    y, qso, so = pl.pallas_call(
        functools.partial(_sample_mixer_kernel, alpha=alpha, nb=nb, ts=ts, bb=bb),
        grid=(nb // bb,),
        in_specs=[
            _const_spec(x.shape), _const_spec(cpre.shape), _const_spec(qst.shape),
            pl.BlockSpec((bb, HEADS, DK, DV), lambda i: (i, 0, 0, 0)),
            pl.BlockSpec((D_MODEL, 2 * QK_DIM), lambda i: (0, qk_blk), pipeline_mode=pl.Buffered(1)),
            pl.BlockSpec((D_MODEL, 2 * V_DIM), lambda i: (0, qk_blk + 1), pipeline_mode=pl.Buffered(1)),
        ] + [_const_spec(a.shape) for a in consts],
        out_specs=[
            pl.BlockSpec(x.shape, lambda i: (0, 0)),
            pl.BlockSpec(qst.shape, lambda i: (0, 0, 0)),
            pl.BlockSpec((bb, HEADS, DK, DV), lambda i: (i, 0, 0, 0)),
        ],
        out_shape=[
            jax.ShapeDtypeStruct(x.shape, F32),
            jax.ShapeDtypeStruct(qst.shape, F32),
            jax.ShapeDtypeStruct(s0.shape, F32),
        ],
        scratch_shapes=[
            pltpu.VMEM((ts * nb, 2 * QK_DIM), F32),
            pltpu.VMEM((ts * nb, 2 * V_DIM), F32),
            pltpu.VMEM((ts * nb, CONV_CH + V_DIM), F32),
            pltpu.VMEM((HEADS, 2 * ts * nb, DK), F32),
            pltpu.VMEM((HEADS, 2 * ts * nb, DV), F32),
            pltpu.VMEM((HEADS, 2 * ts * nb, DK), F32),
            pltpu.VMEM((n_pairs * nb, QK_DIM), F32),
            pltpu.VMEM((HEADS, 2 * ts * nb, DV), F32),
        ],
        compiler_params=pltpu.CompilerParams(
            dimension_semantics=("arbitrary",), vmem_limit_bytes=VMEM_LIMIT),
        name="sample_mixer",
    )(x, cpre, qst, s0, win, win, *consts)
    return y, cso, qso, so


def _row(v):
    return v.reshape(1, -1).astype(F32)


def _layer_params(l, ffn1_w_up, ffn1_w_down, ln1_g, ln1_b, w_in, conv_w, conv_b, conv_ln_g, conv_ln_b,
                  qkv_conv_w, a_log, dt_bias, gdn_norm_g, w_out, ln2_g, ln2_b, ffn2_w_up, ffn2_w_down,
                  ln3_g, ln3_b):
    wsm = jnp.zeros((D_MODEL, LANES), BF16).at[:, 0:2 * HEADS].set(_bf(w_in[l][:, PROJ_MAIN:]))
    gp = jnp.zeros((2, LANES), F32)
    gp = gp.at[0, HEADS:2 * HEADS].set(a_log[l]).at[1, HEADS:2 * HEADS].set(dt_bias[l])
    ffn1 = (_bf(ffn1_w_up[l]), _bf(ffn1_w_down[l]), _row(ln1_g[l]), _row(ln1_b[l]))
    ffn2 = (_bf(ffn2_w_up[l]), _bf(ffn2_w_down[l]), _row(ln3_g[l]), _row(ln3_b[l]))
    mixer = (_bf(w_in[l][:, :PROJ_MAIN]), wsm, conv_w[l], _row(conv_b[l]), _row(conv_ln_g[l]),
             _row(conv_ln_b[l]), qkv_conv_w[l], gp, _row(gdn_norm_g[l]), _bf(w_out[l]),
             _row(ln2_g[l]), _row(ln2_b[l]))
    return ffn1, mixer, ffn2


def kernel(x_prompt, x_sample, state_conv, state_qkv_conv, state_recurrent, ffn1_w_up, ffn1_w_down, ln1_g, ln1_b, w_in, conv_w, conv_b, conv_ln_g, conv_ln_b, qkv_conv_w, a_log, dt_bias, gdn_norm_g, w_out, ln2_g, ln2_b, ffn2_w_up, ffn2_w_down, ln3_g, ln3_b):
    depth = w_in.shape[0]
    alpha = (2.0 * depth) ** 0.25
    bsz, seq, _ = x_prompt.shape
    nb, ts, _ = x_sample.shape
    tm_p = min(512, seq)
    yp = x_prompt
    ys = jnp.swapaxes(x_sample, 0, 1).reshape(ts * nb, D_MODEL)
    outs = [[] for _ in range(6)]
    for l in range(depth):
        ffn1, mixer, ffn2 = _layer_params(
            l, ffn1_w_up, ffn1_w_down, ln1_g, ln1_b, w_in, conv_w, conv_b, conv_ln_g, conv_ln_b,
            qkv_conv_w, a_log, dt_bias, gdn_norm_g, w_out, ln2_g, ln2_b, ffn2_w_up, ffn2_w_down,
            ln3_g, ln3_b)
        x1 = _ffn_ln(yp.reshape(bsz * seq, D_MODEL), *ffn1, alpha=alpha, tm=tm_p)
        x2, c1, q1, s1 = _prompt_mixer(x1.reshape(bsz, seq, D_MODEL), *mixer, alpha=alpha, tm=tm_p)
        yp = _ffn_ln(x2.reshape(bsz * seq, D_MODEL), *ffn2, alpha=alpha, tm=tm_p).reshape(bsz, seq, D_MODEL)
        z1 = _ffn_ln(ys, *ffn1, alpha=alpha, tm=ts * nb)
        z2, c2, q2, s2 = _sample_mixer(
            z1, jnp.swapaxes(state_conv[l], 0, 1), jnp.swapaxes(state_qkv_conv[l], 0, 1),
            state_recurrent[l], *mixer, alpha=alpha, bb=min(8, nb))
        ys = _ffn_ln(z2, *ffn2, alpha=alpha, tm=ts * nb)
        for lst, val in zip(outs, (c1, q1, s1, jnp.swapaxes(c2, 0, 1), jnp.swapaxes(q2, 0, 1), s2)):
            lst.append(val)
    ys = jnp.swapaxes(ys.reshape(ts, nb, D_MODEL), 0, 1)
    return (yp, ys) + tuple(jnp.stack(o) for o in outs)
```

```python
import functools

import jax
import jax.numpy as jnp
from jax import lax
from jax.experimental import pallas as pl
from jax.experimental.pallas import tpu as pltpu

F32 = jnp.float32
BF16 = jnp.bfloat16
HI = lax.Precision.HIGHEST

D_MODEL = 1024
D_FF = 2816
CONV_CH = 512
CONV_WIDTH = 31
HEADS = 4
DK = 128
DV = 128
QK_DIM = HEADS * DK
V_DIM = HEADS * DV
QKV_DIM = 2 * QK_DIM + V_DIM
SHORT_W = 4
CHUNK = 64
PROJ_MAIN = 2 * CONV_CH + QKV_DIM + V_DIM
LN_EPS = 1e-5
RMS_EPS = 1e-6

LANES = 128
SUBLANES = 8
FF_CHUNK = 256
CONV_HALO = 32
QKV_HALO = 8
CONV_ROWS = 64
INV_PREC = None
PREPARE_GROUP = 4
VMEM_LIMIT = 56 * 1024 * 1024


def _dot(a, b, prec=None):
    return jnp.dot(a, b, preferred_element_type=F32, precision=prec)


def _dot_nt(a, b, prec=None):
    return lax.dot_general(a, b, (((1,), (1,)), ((), ())), preferred_element_type=F32, precision=prec)


def _dot_tn(a, b, prec=None):
    return lax.dot_general(a, b, (((0,), (0,)), ((), ())), preferred_element_type=F32, precision=prec)


def _bf(x):
    return x.astype(BF16)


def _layer_norm(x, g, b):
    mu = jnp.mean(x, axis=-1, keepdims=True)
    xc = x - mu
    var = jnp.mean(xc * xc, axis=-1, keepdims=True)
    return xc * lax.rsqrt(var + LN_EPS) * g + b


def _silu(x):
    return x * jax.nn.sigmoid(x)


def _l2norm(x):
    return x * lax.rsqrt(jnp.sum(x * x, axis=-1, keepdims=True) + RMS_EPS)


def _gate_params(sm, gp_ref):
    beta = jax.nn.sigmoid(sm)
    g = -jnp.exp(gp_ref[0:1, :]) * jax.nn.softplus(sm + gp_ref[1:2, :])
    return beta, g


def _gdn_out(o, z, norm_g):
    o = o * lax.rsqrt(jnp.mean(o * o, axis=-1, keepdims=True) + RMS_EPS) * norm_g
    return o * _silu(z)


def _ffn_ln_kernel(x_ref, wup_ref, wdn_ref, g_ref, b_ref, o_ref, xb_ref, acc_ref, *, alpha):
    xb_ref[...] = _bf(x_ref[...])
    for c in range(D_FF // FF_CHUNK):
        lo = c * FF_CHUNK
        a = _dot(xb_ref[...], wup_ref[:, lo:lo + FF_CHUNK])
        b = _dot(xb_ref[...], wup_ref[:, D_FF + lo:D_FF + lo + FF_CHUNK])
        h = _bf(_silu(a) * b)
        d = _dot(h, wdn_ref[lo:lo + FF_CHUNK, :])
        if c == 0:
            acc_ref[...] = d
        else:
            acc_ref[...] += d
    y = alpha * x_ref[...] + 0.5 * acc_ref[...]
    o_ref[...] = _layer_norm(y, g_ref[...], b_ref[...])


def _const_spec(shape):
    zeros = (0,) * len(shape)
    return pl.BlockSpec(shape, lambda *_: zeros, pipeline_mode=pl.Buffered(1))


def _ffn_ln(x, w_up, w_dn, g, b, *, alpha, tm):
    n = x.shape[0]
    assert n % tm == 0
    return pl.pallas_call(
        functools.partial(_ffn_ln_kernel, alpha=alpha),
        grid=(n // tm,),
        in_specs=[
            pl.BlockSpec((tm, D_MODEL), lambda i: (i, 0)),
            _const_spec((D_MODEL, 2 * D_FF)),
            _const_spec((D_FF, D_MODEL)),
            _const_spec((1, D_MODEL)),
            _const_spec((1, D_MODEL)),
        ],
        out_specs=pl.BlockSpec((tm, D_MODEL), lambda i: (i, 0)),
        out_shape=jax.ShapeDtypeStruct((n, D_MODEL), F32),
        scratch_shapes=[pltpu.VMEM((tm, D_MODEL), BF16), pltpu.VMEM((tm, D_MODEL), F32)],
        compiler_params=pltpu.CompilerParams(
            dimension_semantics=("parallel",), vmem_limit_bytes=VMEM_LIMIT),
        name="ffn_ln",
    )(x, w_up, w_dn, g, b)


def _conv_block(uext_ref, cw_ref, r0):
    win = uext_ref[pl.ds(r0, CONV_HALO + CHUNK), :]
    acc = None
    for r in range(SUBLANES):
        rolled = win if r == 0 else pltpu.roll(win, r, 0)
        for a in range(CONV_HALO // SUBLANES):
            lag = SUBLANES * a + r
            if lag >= CONV_WIDTH:
                continue
            lo = CONV_HALO - SUBLANES * a
            j = CONV_WIDTH - 1 - lag
            term = cw_ref[j:j + 1, :] * rolled[lo:lo + CHUNK, :]
            acc = term if acc is None else acc + term
    return acc


def _mm(a, b, prec):
    if prec is None:
        return _dot(_bf(a), _bf(b))
    return _dot(a, b, prec)


def _delta_prepare(group, qn_ref, kn_ref, vv_ref, bn_ref, gn_ref, wq_ref, u_ref, kd_ref, qk_ref, el_ref):
    c = CHUNK
    row = lax.broadcasted_iota(jnp.int32, (c, LANES), 0)
    lane = lax.broadcasted_iota(jnp.int32, (c, LANES), 1)
    low = lane < c
    causal = row >= lane
    strict = row > lane
    eye_hi = jnp.where(lane == row + c, 1.0, 0.0).astype(F32)
    tri = jnp.where(causal[:, :c], 1.0, 0.0).astype(F32)
    tri_t = jnp.where((lane >= row) & low, 1.0, 0.0).astype(F32)
    zeros = jnp.zeros((c, LANES), F32)
    pad = jnp.zeros((c, 2 * LANES), F32)

    chunks = range(PREPARE_GROUP)
    base = group * PREPARE_GROUP
    rows = {ch: pl.ds(pl.multiple_of((base + ch) * c, c), c) for ch in chunks}
    rows2 = {ch: pl.ds(pl.multiple_of((base + ch) * 2 * c, 2 * c), 2 * c) for ch in chunks}
    tile8 = {ch: pl.ds(pl.multiple_of((base + ch) * SUBLANES, SUBLANES), SUBLANES) for ch in chunks}
    g_n ={ch: gn_ref[rows[ch], :] for ch in chunks}
    beta_n = {ch: bn_ref[rows[ch], :] for ch in chunks}
    gcum_n = {ch: _dot(tri, g_n[ch], HI) for ch in chunks}
    gcum_t = {ch: _dot_tn(g_n[ch], tri_t, HI) for ch in chunks}

    items = [(ch, h) for ch in chunks for h in range(HEADS)]
    cs = {it: slice(it[1] * DK, (it[1] + 1) * DK) for it in items}
    gcol = {(ch, h): jnp.broadcast_to(gcum_n[ch][:, HEADS + h:HEADS + h + 1], (c, LANES)) for ch, h in items}
    beta = {(ch, h): jnp.broadcast_to(beta_n[ch][:, h:h + 1], (c, LANES)) for ch, h in items}
    decay = {(ch, h): jnp.exp(jnp.where(causal, gcol[ch, h] - gcum_t[ch][HEADS + h:HEADS + h + 1, :], -jnp.inf))
             for ch, h in items}
    k = {it: kn_ref[rows[it[0]], cs[it]] for it in items}
    q = {it: qn_ref[rows[it[0]], cs[it]] for it in items}
    kb = {it: k[it] * beta[it] for it in items}
    kq = {it: _dot_nt(_bf(jnp.concatenate([kb[it], q[it]], axis=0)), _bf(jnp.concatenate([k[it], zeros], axis=0)))
          for it in items}
    z = {it: jnp.where(low, jnp.where(strict, -kq[it][:c] * decay[it], 0.0), eye_hi) for it in items}
    for _ in range(c.bit_length() - 1):
        z = {it: _mm(z[it][:, :c], z[it], INV_PREC) + jnp.where(low, 0.0, z[it]) for it in items}
    eg = {it: jnp.exp(gcol[it]) for it in items}
    rhs = {it: jnp.concatenate([vv_ref[rows[it[0]], cs[it]] * beta[it], kb[it] * eg[it]], axis=1) for it in items}
    sol = {it: _mm(z[it], jnp.concatenate([pad, rhs[it]], axis=0), INV_PREC) for it in items}
    for it in items:
        ch = it[0]
        g_last = gcol[it][c - 1:c, :]
        wq_ref[rows2[ch], cs[it]] = jnp.concatenate([sol[it][:, DV:], q[it] * eg[it]], axis=0)
        u_ref[rows[ch], cs[it]] = sol[it][:, :DV]
        kd_ref[rows[ch], cs[it]] = k[it] * jnp.exp(g_last - gcol[it])
        qk_ref[rows[ch], cs[it]] = kq[it][c:] * decay[it]
        el_ref[tile8[ch], cs[it]] = jnp.broadcast_to(jnp.exp(g_last), (SUBLANES, LANES))


def _delta_scan_chunk(rows, rows2, tile8, wq_ref, u_ref, kd_ref, qk_ref, el_ref, s_ref):
    c = CHUNK
    heads = range(HEADS)
    cs = [slice(h * DK, (h + 1) * DK) for h in heads]
    s = [s_ref[h] for h in heads]
    r = [_dot(_bf(wq_ref[rows2, cs[h]]), _bf(s[h])) for h in heads]
    v_new = [u_ref[rows, cs[h]] - r[h][:c] for h in heads]
    o = [r[h][c:] + _dot(_bf(qk_ref[rows, cs[h]][:, :c]), _bf(v_new[h])) for h in heads]
    for h in heads:
        s_ref[h] = s[h] * el_ref[tile8, cs[h]][0:1, :] + _dot_tn(_bf(kd_ref[rows, cs[h]]), _bf(v_new[h]))
    return o


def _prompt_mixer_kernel(x_ref, win_ref, wsm_ref, cw_ref, cb_ref, cg_ref, cbb_ref, qw_ref, gp_ref,
                         ng_ref, wout_ref, lg_ref, lb_ref,
                         y_ref, cst_ref, qst_ref, sst_ref,
                         hs_ref, uext_ref, qext_ref, qn_ref, kn_ref, vv_ref, bn_ref, gn_ref, mix_ref,
                         wq_ref, u_ref, kd_ref, qk_ref, el_ref, s_ref, *, alpha, tm):
    t = pl.program_id(1)

    @pl.when(t == 0)
    def _():
        uext_ref[0:CONV_HALO, :] = jnp.zeros((CONV_HALO, CONV_CH), F32)
        qext_ref[0:QKV_HALO, :] = jnp.zeros((QKV_HALO, QKV_DIM), F32)
        s_ref[...] = jnp.zeros_like(s_ref)

    xb = _bf(x_ref[0])
    hs_ref[...] = _dot(xb, win_ref[...])
    sm = _dot(xb, wsm_ref[...])

    uext_ref[CONV_HALO:CONV_HALO + tm, :] = hs_ref[:, 0:CONV_CH] * jax.nn.sigmoid(hs_ref[:, CONV_CH:2 * CONV_CH])
    off = CONV_HALO - (CONV_WIDTH - 1)

    qext_ref[QKV_HALO:QKV_HALO + tm, :] = hs_ref[:, 2 * CONV_CH:2 * CONV_CH + QKV_DIM]
    qoff = QKV_HALO - (SHORT_W - 1)
    for r0 in range(0, tm, CONV_ROWS):
        acc = qw_ref[0:1, :] * qext_ref[qoff + r0:qoff + r0 + CONV_ROWS, :]
        for j in range(1, SHORT_W):
            acc = acc + qw_ref[j:j + 1, :] * qext_ref[qoff + r0 + j:qoff + r0 + j + CONV_ROWS, :]
        acc = _silu(acc)
        rows = slice(r0, r0 + CONV_ROWS)
        for h in range(HEADS):
            cs = slice(h * DK, (h + 1) * DK)
            qn_ref[rows, cs] = _l2norm(acc[:, h * DK:(h + 1) * DK]) * (DK ** -0.5)
            kn_ref[rows, cs] = _l2norm(acc[:, QK_DIM + h * DK:QK_DIM + (h + 1) * DK])
        vv_ref[rows, :] = acc[:, 2 * QK_DIM:]

    bn_ref[...], gn_ref[...] = _gate_params(sm, gp_ref)

    def chunk_slices(c):
        rows = pl.ds(pl.multiple_of(c * CHUNK, CHUNK), CHUNK)
        rows2 = pl.ds(pl.multiple_of(c * 2 * CHUNK, 2 * CHUNK), 2 * CHUNK)
        tile8 = pl.ds(pl.multiple_of(c * SUBLANES, SUBLANES), SUBLANES)
        return rows, rows2, tile8

    def prepare_body(group, carry):
        _delta_prepare(group, qn_ref, kn_ref, vv_ref, bn_ref, gn_ref, wq_ref, u_ref, kd_ref, qk_ref, el_ref)
        return carry

    lax.fori_loop(0, tm // (CHUNK * PREPARE_GROUP), prepare_body, 0)

    def scan_body(c, carry):
        rows, rows2, tile8 = chunk_slices(c)
        cc = _conv_block(uext_ref, cw_ref, pl.multiple_of(c * CHUNK, CHUNK)) + cb_ref[...]
        mix_ref[rows, 0:CONV_CH] = _silu(_layer_norm(cc, cg_ref[...], cbb_ref[...]))
        o = _delta_scan_chunk(rows, rows2, tile8, wq_ref, u_ref, kd_ref, qk_ref, el_ref, s_ref)
        for h in range(HEADS):
            zc = slice(2 * CONV_CH + QKV_DIM + h * DV, 2 * CONV_CH + QKV_DIM + (h + 1) * DV)
            mix_ref[rows, CONV_CH + h * DV:CONV_CH + (h + 1) * DV] = _gdn_out(o[h], hs_ref[rows, zc], ng_ref[...])
        return carry

    lax.fori_loop(0, tm // CHUNK, scan_body, 0)

    y = alpha * x_ref[0] + _dot(_bf(mix_ref[...]), wout_ref[...])
    y_ref[0] = _layer_norm(y, lg_ref[...], lb_ref[...])

    uext_ref[0:CONV_HALO, :] = uext_ref[tm:tm + CONV_HALO, :]
    qext_ref[0:QKV_HALO, :] = qext_ref[tm:tm + QKV_HALO, :]

    @pl.when(t == pl.num_programs(1) - 1)
    def _():
        cst_ref[0] = uext_ref[off:CONV_HALO, :]
        qst_ref[0] = qext_ref[qoff:QKV_HALO, :]
        sst_ref[0] = s_ref[...]


def _prompt_mixer(x, win, wsm, cw, cb, cg, cbb, qw, gp, ng, wout, lg, lb, *, alpha, tm):
    bsz, seq, _ = x.shape
    assert seq % tm == 0 and tm % (CHUNK * PREPARE_GROUP) == 0 and tm >= CONV_HALO
    consts = [win, wsm, cw, cb, cg, cbb, qw, gp, ng, wout, lg, lb]
    return pl.pallas_call(
        functools.partial(_prompt_mixer_kernel, alpha=alpha, tm=tm),
        grid=(bsz, seq // tm),
        in_specs=[pl.BlockSpec((1, tm, D_MODEL), lambda b, t: (b, t, 0))]
        + [_const_spec(a.shape) for a in consts],
        out_specs=[
            pl.BlockSpec((1, tm, D_MODEL), lambda b, t: (b, t, 0)),
            pl.BlockSpec((1, CONV_WIDTH - 1, CONV_CH), lambda b, t: (b, 0, 0)),
            pl.BlockSpec((1, SHORT_W - 1, QKV_DIM), lambda b, t: (b, 0, 0)),
            pl.BlockSpec((1, HEADS, DK, DV), lambda b, t: (b, 0, 0, 0)),
        ],
        out_shape=[
            jax.ShapeDtypeStruct((bsz, seq, D_MODEL), F32),
            jax.ShapeDtypeStruct((bsz, CONV_WIDTH - 1, CONV_CH), F32),
            jax.ShapeDtypeStruct((bsz, SHORT_W - 1, QKV_DIM), F32),
            jax.ShapeDtypeStruct((bsz, HEADS, DK, DV), F32),
        ],
        scratch_shapes=[
            pltpu.VMEM((tm, PROJ_MAIN), F32),
            pltpu.VMEM((CONV_HALO + tm, CONV_CH), F32),
            pltpu.VMEM((QKV_HALO + tm, QKV_DIM), F32),
            pltpu.VMEM((tm, QK_DIM), F32),
            pltpu.VMEM((tm, QK_DIM), F32),
            pltpu.VMEM((tm, V_DIM), F32),
            pltpu.VMEM((tm, LANES), F32),
            pltpu.VMEM((tm, LANES), F32),
            pltpu.VMEM((tm, CONV_CH + V_DIM), F32),
            pltpu.VMEM((2 * tm, QK_DIM), F32),
            pltpu.VMEM((tm, V_DIM), F32),
            pltpu.VMEM((tm, QK_DIM), F32),
            pltpu.VMEM((tm, QK_DIM), F32),
            pltpu.VMEM((tm // CHUNK * SUBLANES, QK_DIM), F32),
            pltpu.VMEM((HEADS, DK, DV), F32),
        ],
        compiler_params=pltpu.CompilerParams(
            dimension_semantics=("parallel", "arbitrary"), vmem_limit_bytes=VMEM_LIMIT),
        name="prompt_mixer",
    )(x, *consts)


def _sample_conv_kernel(x_ref, wa_ref, wg_ref, cst_ref, cw_ref, cpre_ref, cso_ref, *, nb, ts):
    xb = _bf(x_ref[...])
    glu_all = _dot(xb, wa_ref[...]) * jax.nn.sigmoid(_dot(xb, wg_ref[...]))
    glu = [glu_all[t * nb:(t + 1) * nb, :] for t in range(ts)]

    def conv_in(s):
        return cst_ref[s] if s < CONV_WIDTH - 1 else glu[s - (CONV_WIDTH - 1)]

    for t in range(ts):
        acc = cw_ref[0:1, :] * conv_in(t)
        for j in range(1, CONV_WIDTH):
            acc = acc + cw_ref[j:j + 1, :] * conv_in(t + j)
        cpre_ref[t * nb:(t + 1) * nb, :] = acc
    for s in range(CONV_WIDTH - 1):
        cso_ref[s] = conv_in(s + ts)


def _sample_conv(x, win, cst, cw, *, nb, ts):
    n_blk = CONV_CH // LANES
    return pl.pallas_call(
        functools.partial(_sample_conv_kernel, nb=nb, ts=ts),
        grid=(n_blk,),
        in_specs=[
            _const_spec(x.shape),
            pl.BlockSpec((D_MODEL, LANES), lambda j: (0, j)),
            pl.BlockSpec((D_MODEL, LANES), lambda j: (0, n_blk + j)),
            pl.BlockSpec((CONV_WIDTH - 1, nb, LANES), lambda j: (0, 0, j)),
            pl.BlockSpec((CONV_WIDTH, LANES), lambda j: (0, j)),
        ],
        out_specs=[
            pl.BlockSpec((ts * nb, LANES), lambda j: (0, j)),
            pl.BlockSpec((CONV_WIDTH - 1, nb, LANES), lambda j: (0, 0, j)),
        ],
        out_shape=[
            jax.ShapeDtypeStruct((ts * nb, CONV_CH), F32),
            jax.ShapeDtypeStruct(cst.shape, F32),
        ],
        compiler_params=pltpu.CompilerParams(
            dimension_semantics=("parallel",), vmem_limit_bytes=VMEM_LIMIT),
        name="sample_conv",
    )(x, win, win, cst, cw)


def _sample_mixer_kernel(x_ref, cpre_ref, qst_ref, s_ref, wqk_ref, wvz_ref, wsm_ref, cb_ref, cg_ref,
                         cbb_ref, qw_ref, gp_ref, ng_ref, wout_ref, lg_ref, lb_ref,
                         y_ref, qso_ref, so_ref,
                         hqk_ref, hvz_ref, mix_ref, lw_ref, u_ref, kd_ref, qk_ref, rs_ref,
                         *, alpha, nb, ts, bb):
    i = pl.program_id(0)
    n_slab = 2 * ts
    pairs = [(a, b) for a in range(ts) for b in range(a + 1)]

    def slab(t):
        return slice(t * nb, (t + 1) * nb)

    @pl.when(i == 0)
    def _():
        xb = _bf(x_ref[...])
        hqk_ref[...] = _dot(xb, wqk_ref[...])
        hvz_ref[...] = _dot(xb, wvz_ref[...])
        sm = _dot(xb, wsm_ref[...])

        cc = _layer_norm(cpre_ref[...] + cb_ref[...], cg_ref[...], cbb_ref[...])
        mix_ref[:, 0:CONV_CH] = _silu(cc)

        for s in range(SHORT_W - 1):
            t = s + ts - (SHORT_W - 1)
            qso_ref[s, :, 0:2 * QK_DIM] = hqk_ref[slab(t), :]
            qso_ref[s, :, 2 * QK_DIM:QKV_DIM] = hvz_ref[slab(t), 0:V_DIM]

        def short_conv(col, src_ref, src_col):
            outs = []
            for t in range(ts):
                acc = None
                for j in range(SHORT_W):
                    s = t + j
                    if s < SHORT_W - 1:
                        xin = qst_ref[s, :, col:col + LANES]
                    else:
                        xin = src_ref[slab(s - (SHORT_W - 1)), src_col:src_col + LANES]
                    term = qw_ref[j:j + 1, col:col + LANES] * xin
                    acc = term if acc is None else acc + term
                outs.append(_silu(acc))
            return outs

        beta_all, g_all = _gate_params(sm, gp_ref)
        zeros = jnp.zeros((nb, LANES), F32)
        for h in range(HEADS):
            cs = slice(h * DK, (h + 1) * DK)
            q = [_l2norm(a) * (DK ** -0.5) for a in short_conv(h * DK, hqk_ref, h * DK)]
            k = [_l2norm(a) for a in short_conv(QK_DIM + h * DK, hqk_ref, QK_DIM + h * DK)]
            v = short_conv(2 * QK_DIM + h * DV, hvz_ref, h * DV)
            beta = [jnp.broadcast_to(beta_all[slab(t), h:h + 1], (nb, LANES)) for t in range(ts)]
            gcum = []
            for t in range(ts):
                gt = jnp.broadcast_to(g_all[slab(t), HEADS + h:HEADS + h + 1], (nb, LANES))
                gcum.append(gt if t == 0 else gcum[-1] + gt)
            us, ws = [], []
            for a in range(ts):
                ua = v[a] * beta[a]
                wa = k[a] * beta[a] * jnp.exp(gcum[a])
                for b in range(a):
                    a_ab = beta[a] * jnp.sum(k[a] * k[b], axis=-1, keepdims=True) * jnp.exp(gcum[a] - gcum[b])
                    ua = ua - a_ab * us[b]
                    wa = wa - a_ab * ws[b]
                us.append(ua)
                ws.append(wa)
            for t in range(ts):
                lw_ref[h, slab(t), :] = ws[t]
                lw_ref[h, slab(ts + t), :] = q[t] * jnp.exp(gcum[t])
                u_ref[h, slab(t), :] = us[t]
                u_ref[h, slab(ts + t), :] = zeros
                kd_ref[h, slab(t), :] = k[t] * jnp.exp(gcum[ts - 1] - gcum[t])
                kd_ref[h, slab(ts + t), :] = jnp.exp(gcum[ts - 1]) if t == 0 else zeros
            for p, (a, b) in enumerate(pairs):
                qk_ref[slab(p), cs] = jnp.sum(q[a] * k[b], axis=-1, keepdims=True) * jnp.exp(gcum[a] - gcum[b])

    def seq_body(bl, carry):
        b = i * bb + bl
        rows = pl.ds(b, n_slab, stride=nb)
        keep = lax.broadcasted_iota(jnp.int32, (n_slab, LANES), 0) < ts
        for h in range(HEADS):
            cs = slice(h * DK, (h + 1) * DK)
            s = s_ref[bl, h]
            r = _dot(_bf(lw_ref[h, rows, :]), _bf(s))
            rs_ref[h, rows, :] = r
            v_new = jnp.where(keep, u_ref[h, rows, :] - r, 0.0)
            kd = kd_ref[h, rows, :]
            so_ref[bl, h] = s * kd[ts:ts + 1, :] + _dot_tn(_bf(kd), _bf(v_new))
        return carry

    lax.fori_loop(0, bb, seq_body, 0)

    @pl.when(i == pl.num_programs(0) - 1)
    def _():
        for h in range(HEADS):
            cs = slice(h * DK, (h + 1) * DK)
            v_new = [u_ref[h, slab(t), :] - rs_ref[h, slab(t), :] for t in range(ts)]
            for a in range(ts):
                o = rs_ref[h, slab(ts + a), :]
                for b in range(a + 1):
                    o = o + qk_ref[slab(pairs.index((a, b))), cs] * v_new[b]
                zc = slice(V_DIM + h * DV, V_DIM + (h + 1) * DV)
                mix_ref[slab(a), CONV_CH + h * DV:CONV_CH + (h + 1) * DV] = _gdn_out(
                    o, hvz_ref[slab(a), zc], ng_ref[...])
        y = alpha * x_ref[...] + _dot(_bf(mix_ref[...]), wout_ref[...])
        y_ref[...] = _layer_norm(y, lg_ref[...], lb_ref[...])


def _sample_mixer(x, cst, qst, s0, win, wsm, cw, cb, cg, cbb, qw, gp, ng, wout, lg, lb, *, alpha, bb):
    nb = s0.shape[0]
    ts = x.shape[0] // nb
    assert nb % bb == 0 and nb % SUBLANES == 0 and ts >= SHORT_W - 1
    cpre, cso = _sample_conv(x, win, cst, cw, nb=nb, ts=ts)
    consts = [wsm, cb, cg, cbb, qw, gp, ng, wout, lg, lb]
    n_pairs = ts * (ts + 1) // 2
    qk_blk = 2 * CONV_CH // (2 * QK_DIM)
    y, qso, so = pl.pallas_call(
        functools.partial(_sample_mixer_kernel, alpha=alpha, nb=nb, ts=ts, bb=bb),
        grid=(nb // bb,),
        in_specs=[
            _const_spec(x.shape), _const_spec(cpre.shape), _const_spec(qst.shape),
            pl.BlockSpec((bb, HEADS, DK, DV), lambda i: (i, 0, 0, 0)),
            pl.BlockSpec((D_MODEL, 2 * QK_DIM), lambda i: (0, qk_blk), pipeline_mode=pl.Buffered(1)),
            pl.BlockSpec((D_MODEL, 2 * V_DIM), lambda i: (0, qk_blk + 1), pipeline_mode=pl.Buffered(1)),
        ] + [_const_spec(a.shape) for a in consts],
        out_specs=[
            pl.BlockSpec(x.shape, lambda i: (0, 0)),
            pl.BlockSpec(qst.shape, lambda i: (0, 0, 0)),
            pl.BlockSpec((bb, HEADS, DK, DV), lambda i: (i, 0, 0, 0)),
        ],
        out_shape=[
            jax.ShapeDtypeStruct(x.shape, F32),
            jax.ShapeDtypeStruct(qst.shape, F32),
            jax.ShapeDtypeStruct(s0.shape, F32),
        ],
        scratch_shapes=[
            pltpu.VMEM((ts * nb, 2 * QK_DIM), F32),
            pltpu.VMEM((ts * nb, 2 * V_DIM), F32),
            pltpu.VMEM((ts * nb, CONV_CH + V_DIM), F32),
            pltpu.VMEM((HEADS, 2 * ts * nb, DK), F32),
            pltpu.VMEM((HEADS, 2 * ts * nb, DV), F32),
            pltpu.VMEM((HEADS, 2 * ts * nb, DK), F32),
            pltpu.VMEM((n_pairs * nb, QK_DIM), F32),
            pltpu.VMEM((HEADS, 2 * ts * nb, DV), F32),
        ],
        compiler_params=pltpu.CompilerParams(
            dimension_semantics=("arbitrary",), vmem_limit_bytes=VMEM_LIMIT),
        name="sample_mixer",
    )(x, cpre, qst, s0, win, win, *consts)
    return y, cso, qso, so


def _row(v):
    return v.reshape(1, -1).astype(F32)


def _layer_params(l, ffn1_w_up, ffn1_w_down, ln1_g, ln1_b, w_in, conv_w, conv_b, conv_ln_g, conv_ln_b,
                  qkv_conv_w, a_log, dt_bias, gdn_norm_g, w_out, ln2_g, ln2_b, ffn2_w_up, ffn2_w_down,
                  ln3_g, ln3_b):
    wsm = jnp.zeros((D_MODEL, LANES), BF16).at[:, 0:2 * HEADS].set(_bf(w_in[l][:, PROJ_MAIN:]))
    gp = jnp.zeros((2, LANES), F32)
    gp = gp.at[0, HEADS:2 * HEADS].set(a_log[l]).at[1, HEADS:2 * HEADS].set(dt_bias[l])
    ffn1 = (_bf(ffn1_w_up[l]), _bf(ffn1_w_down[l]), _row(ln1_g[l]), _row(ln1_b[l]))
    ffn2 = (_bf(ffn2_w_up[l]), _bf(ffn2_w_down[l]), _row(ln3_g[l]), _row(ln3_b[l]))
    mixer = (_bf(w_in[l][:, :PROJ_MAIN]), wsm, conv_w[l], _row(conv_b[l]), _row(conv_ln_g[l]),
             _row(conv_ln_b[l]), qkv_conv_w[l], gp, _row(gdn_norm_g[l]), _bf(w_out[l]),
             _row(ln2_g[l]), _row(ln2_b[l]))
    return ffn1, mixer, ffn2


def kernel(x_prompt, x_sample, state_conv, state_qkv_conv, state_recurrent, ffn1_w_up, ffn1_w_down, ln1_g, ln1_b, w_in, conv_w, conv_b, conv_ln_g, conv_ln_b, qkv_conv_w, a_log, dt_bias, gdn_norm_g, w_out, ln2_g, ln2_b, ffn2_w_up, ffn2_w_down, ln3_g, ln3_b):
    depth = w_in.shape[0]
    alpha = (2.0 * depth) ** 0.25
    bsz, seq, _ = x_prompt.shape
    nb, ts, _ = x_sample.shape
    tm_p = min(512, seq)
    yp = x_prompt
    ys = jnp.swapaxes(x_sample, 0, 1).reshape(ts * nb, D_MODEL)
    outs = [[] for _ in range(6)]
    for l in range(depth):
        ffn1, mixer, ffn2 = _layer_params(
            l, ffn1_w_up, ffn1_w_down, ln1_g, ln1_b, w_in, conv_w, conv_b, conv_ln_g, conv_ln_b,
            qkv_conv_w, a_log, dt_bias, gdn_norm_g, w_out, ln2_g, ln2_b, ffn2_w_up, ffn2_w_down,
            ln3_g, ln3_b)
        x1 = _ffn_ln(yp.reshape(bsz * seq, D_MODEL), *ffn1, alpha=alpha, tm=tm_p)
        x2, c1, q1, s1 = _prompt_mixer(x1.reshape(bsz, seq, D_MODEL), *mixer, alpha=alpha, tm=tm_p)
        yp = _ffn_ln(x2.reshape(bsz * seq, D_MODEL), *ffn2, alpha=alpha, tm=tm_p).reshape(bsz, seq, D_MODEL)
        z1 = _ffn_ln(ys, *ffn1, alpha=alpha, tm=ts * nb)
        z2, c2, q2, s2 = _sample_mixer(
            z1, jnp.swapaxes(state_conv[l], 0, 1), jnp.swapaxes(state_qkv_conv[l], 0, 1),
            state_recurrent[l], *mixer, alpha=alpha, bb=min(8, nb))
        ys = _ffn_ln(z2, *ffn2, alpha=alpha, tm=ts * nb)
        for lst, val in zip(outs, (c1, q1, s1, jnp.swapaxes(c2, 0, 1), jnp.swapaxes(q2, 0, 1), s2)):
            lst.append(val)
    ys = jnp.swapaxes(ys.reshape(ts, nb, D_MODEL), 0, 1)
    return (yp, ys) + tuple(jnp.stack(o) for o in outs)
```

```python
import functools

import jax
import jax.numpy as jnp
from jax import lax
from jax.experimental import pallas as pl
from jax.experimental.pallas import tpu as pltpu

F32 = jnp.float32
BF16 = jnp.bfloat16
HI = lax.Precision.HIGHEST

D_MODEL = 1024
D_FF = 2816
CONV_CH = 512
CONV_WIDTH = 31
HEADS = 4
DK = 128
DV = 128
QK_DIM = HEADS * DK
V_DIM = HEADS * DV
QKV_DIM = 2 * QK_DIM + V_DIM
SHORT_W = 4
CHUNK = 64
PROJ_MAIN = 2 * CONV_CH + QKV_DIM + V_DIM
LN_EPS = 1e-5
RMS_EPS = 1e-6

LANES = 128
SUBLANES = 8
FF_CHUNK = 256
CONV_HALO = 32
QKV_HALO = 8
CONV_ROWS = 64
INV_PREC = None
SEQ_GROUP = 4
PREPARE_GROUP = 4
VMEM_LIMIT = 56 * 1024 * 1024


def _dot(a, b, prec=None):
    return jnp.dot(a, b, preferred_element_type=F32, precision=prec)


def _dot_nt(a, b, prec=None):
    return lax.dot_general(a, b, (((1,), (1,)), ((), ())), preferred_element_type=F32, precision=prec)


def _dot_tn(a, b, prec=None):
    return lax.dot_general(a, b, (((0,), (0,)), ((), ())), preferred_element_type=F32, precision=prec)


def _bf(x):
    return x.astype(BF16)


def _layer_norm(x, g, b):
    mu = jnp.mean(x, axis=-1, keepdims=True)
    xc = x - mu
    var = jnp.mean(xc * xc, axis=-1, keepdims=True)
    return xc * lax.rsqrt(var + LN_EPS) * g + b


def _silu(x):
    return x * jax.nn.sigmoid(x)


def _l2norm(x):
    return x * lax.rsqrt(jnp.sum(x * x, axis=-1, keepdims=True) + RMS_EPS)


def _gate_params(sm, gp_ref):
    beta = jax.nn.sigmoid(sm)
    g = -jnp.exp(gp_ref[0:1, :]) * jax.nn.softplus(sm + gp_ref[1:2, :])
    return beta, g


def _gdn_out(o, z, norm_g):
    o = o * lax.rsqrt(jnp.mean(o * o, axis=-1, keepdims=True) + RMS_EPS) * norm_g
    return o * _silu(z)


def _ffn_ln_kernel(x_ref, wup_ref, wdn_ref, g_ref, b_ref, o_ref, xb_ref, acc_ref, *, alpha):
    xb_ref[...] = _bf(x_ref[...])
    for c in range(D_FF // FF_CHUNK):
        lo = c * FF_CHUNK
        a = _dot(xb_ref[...], wup_ref[:, lo:lo + FF_CHUNK])
        b = _dot(xb_ref[...], wup_ref[:, D_FF + lo:D_FF + lo + FF_CHUNK])
        h = _bf(_silu(a) * b)
        d = _dot(h, wdn_ref[lo:lo + FF_CHUNK, :])
        if c == 0:
            acc_ref[...] = d
        else:
            acc_ref[...] += d
    y = alpha * x_ref[...] + 0.5 * acc_ref[...]
    o_ref[...] = _layer_norm(y, g_ref[...], b_ref[...])


def _const_spec(shape):
    zeros = (0,) * len(shape)
    return pl.BlockSpec(shape, lambda *_: zeros, pipeline_mode=pl.Buffered(1))


def _ffn_ln(x, w_up, w_dn, g, b, *, alpha, tm):
    n = x.shape[0]
    assert n % tm == 0
    return pl.pallas_call(
        functools.partial(_ffn_ln_kernel, alpha=alpha),
        grid=(n // tm,),
        in_specs=[
            pl.BlockSpec((tm, D_MODEL), lambda i: (i, 0)),
            _const_spec((D_MODEL, 2 * D_FF)),
            _const_spec((D_FF, D_MODEL)),
            _const_spec((1, D_MODEL)),
            _const_spec((1, D_MODEL)),
        ],
        out_specs=pl.BlockSpec((tm, D_MODEL), lambda i: (i, 0)),
        out_shape=jax.ShapeDtypeStruct((n, D_MODEL), F32),
        scratch_shapes=[pltpu.VMEM((tm, D_MODEL), BF16), pltpu.VMEM((tm, D_MODEL), F32)],
        compiler_params=pltpu.CompilerParams(
            dimension_semantics=("parallel",), vmem_limit_bytes=VMEM_LIMIT),
        name="ffn_ln",
    )(x, w_up, w_dn, g, b)


def _conv_block(uext_ref, cw_ref, r0):
    win = uext_ref[pl.ds(r0, CONV_HALO + CHUNK), :]
    acc = None
    for r in range(SUBLANES):
        rolled = win if r == 0 else pltpu.roll(win, r, 0)
        for a in range(CONV_HALO // SUBLANES):
            lag = SUBLANES * a + r
            if lag >= CONV_WIDTH:
                continue
            lo = CONV_HALO - SUBLANES * a
            j = CONV_WIDTH - 1 - lag
            term = cw_ref[j:j + 1, :] * rolled[lo:lo + CHUNK, :]
            acc = term if acc is None else acc + term
    return acc


def _mm(a, b, prec):
    if prec is None:
        return _dot(_bf(a), _bf(b))
    return _dot(a, b, prec)


def _delta_prepare(group, qn_ref, kn_ref, vv_ref, bn_ref, gn_ref, wq_ref, u_ref, kd_ref, qk_ref, el_ref):
    c = CHUNK
    row = lax.broadcasted_iota(jnp.int32, (c, LANES), 0)
    lane = lax.broadcasted_iota(jnp.int32, (c, LANES), 1)
    low = lane < c
    causal = row >= lane
    strict = row > lane
    eye_hi = jnp.where(lane == row + c, 1.0, 0.0).astype(F32)
    tri = jnp.where(causal[:, :c], 1.0, 0.0).astype(F32)
    tri_t = jnp.where((lane >= row) & low, 1.0, 0.0).astype(F32)
    zeros = jnp.zeros((c, LANES), F32)
    pad = jnp.zeros((c, 2 * LANES), F32)

    chunks = range(PREPARE_GROUP)
    base = group * PREPARE_GROUP
    rows = {ch: pl.ds(pl.multiple_of((base + ch) * c, c), c) for ch in chunks}
    rows2 = {ch: pl.ds(pl.multiple_of((base + ch) * 2 * c, 2 * c), 2 * c) for ch in chunks}
    tile8 = {ch: pl.ds(pl.multiple_of((base + ch) * SUBLANES, SUBLANES), SUBLANES) for ch in chunks}
    g_n ={ch: gn_ref[rows[ch], :] for ch in chunks}
    beta_n = {ch: bn_ref[rows[ch], :] for ch in chunks}
    gcum_n = {ch: _dot(tri, g_n[ch], HI) for ch in chunks}
    gcum_t = {ch: _dot_tn(g_n[ch], tri_t, HI) for ch in chunks}

    items = [(ch, h) for ch in chunks for h in range(HEADS)]
    cs = {it: slice(it[1] * DK, (it[1] + 1) * DK) for it in items}
    gcol = {(ch, h): jnp.broadcast_to(gcum_n[ch][:, HEADS + h:HEADS + h + 1], (c, LANES)) for ch, h in items}
    beta = {(ch, h): jnp.broadcast_to(beta_n[ch][:, h:h + 1], (c, LANES)) for ch, h in items}
    decay = {(ch, h): jnp.exp(jnp.where(causal, gcol[ch, h] - gcum_t[ch][HEADS + h:HEADS + h + 1, :], -jnp.inf))
             for ch, h in items}
    k = {it: kn_ref[rows[it[0]], cs[it]] for it in items}
    q = {it: qn_ref[rows[it[0]], cs[it]] for it in items}
    kb = {it: k[it] * beta[it] for it in items}
    kq = {it: _dot_nt(_bf(jnp.concatenate([kb[it], q[it]], axis=0)), _bf(jnp.concatenate([k[it], zeros], axis=0)))
          for it in items}
    z = {it: jnp.where(low, jnp.where(strict, -kq[it][:c] * decay[it], 0.0), eye_hi) for it in items}
    for _ in range(c.bit_length() - 1):
        z = {it: _mm(z[it][:, :c], z[it], INV_PREC) + jnp.where(low, 0.0, z[it]) for it in items}
    eg = {it: jnp.exp(gcol[it]) for it in items}
    rhs = {it: jnp.concatenate([vv_ref[rows[it[0]], cs[it]] * beta[it], kb[it] * eg[it]], axis=1) for it in items}
    sol = {it: _mm(z[it], jnp.concatenate([pad, rhs[it]], axis=0), INV_PREC) for it in items}
    for it in items:
        ch = it[0]
        g_last = gcol[it][c - 1:c, :]
        wq_ref[rows2[ch], cs[it]] = jnp.concatenate([sol[it][:, DV:], q[it] * eg[it]], axis=0)
        u_ref[rows[ch], cs[it]] = sol[it][:, :DV]
        kd_ref[rows[ch], cs[it]] = k[it] * jnp.exp(g_last - gcol[it])
        qk_ref[rows[ch], cs[it]] = kq[it][c:] * decay[it]
        el_ref[tile8[ch], cs[it]] = jnp.broadcast_to(jnp.exp(g_last), (SUBLANES, LANES))


def _delta_scan_chunk(rows, rows2, tile8, wq_ref, u_ref, kd_ref, qk_ref, el_ref, s_ref):
    c = CHUNK
    heads = range(HEADS)
    cs = [slice(h * DK, (h + 1) * DK) for h in heads]
    s = [s_ref[h] for h in heads]
    r = [_dot(_bf(wq_ref[rows2, cs[h]]), _bf(s[h])) for h in heads]
    v_new = [u_ref[rows, cs[h]] - r[h][:c] for h in heads]
    o = [r[h][c:] + _dot(_bf(qk_ref[rows, cs[h]][:, :c]), _bf(v_new[h])) for h in heads]
    for h in heads:
        s_ref[h] = s[h] * el_ref[tile8, cs[h]][0:1, :] + _dot_tn(_bf(kd_ref[rows, cs[h]]), _bf(v_new[h]))
    return o


def _prompt_mixer_kernel(x_ref, win_ref, wsm_ref, cw_ref, cb_ref, cg_ref, cbb_ref, qw_ref, gp_ref,
                         ng_ref, wout_ref, lg_ref, lb_ref,
                         y_ref, cst_ref, qst_ref, sst_ref,
                         hs_ref, uext_ref, qext_ref, qn_ref, kn_ref, vv_ref, bn_ref, gn_ref, mix_ref,
                         wq_ref, u_ref, kd_ref, qk_ref, el_ref, s_ref, *, alpha, tm):
    t = pl.program_id(1)

    @pl.when(t == 0)
    def _():
        uext_ref[0:CONV_HALO, :] = jnp.zeros((CONV_HALO, CONV_CH), F32)
        qext_ref[0:QKV_HALO, :] = jnp.zeros((QKV_HALO, QKV_DIM), F32)
        s_ref[...] = jnp.zeros_like(s_ref)

    xb = _bf(x_ref[0])
    hs_ref[...] = _dot(xb, win_ref[...])
    sm = _dot(xb, wsm_ref[...])

    uext_ref[CONV_HALO:CONV_HALO + tm, :] = hs_ref[:, 0:CONV_CH] * jax.nn.sigmoid(hs_ref[:, CONV_CH:2 * CONV_CH])
    off = CONV_HALO - (CONV_WIDTH - 1)

    qext_ref[QKV_HALO:QKV_HALO + tm, :] = hs_ref[:, 2 * CONV_CH:2 * CONV_CH + QKV_DIM]
    qoff = QKV_HALO - (SHORT_W - 1)
    for r0 in range(0, tm, CONV_ROWS):
        acc = qw_ref[0:1, :] * qext_ref[qoff + r0:qoff + r0 + CONV_ROWS, :]
        for j in range(1, SHORT_W):
            acc = acc + qw_ref[j:j + 1, :] * qext_ref[qoff + r0 + j:qoff + r0 + j + CONV_ROWS, :]
        acc = _silu(acc)
        rows = slice(r0, r0 + CONV_ROWS)
        for h in range(HEADS):
            cs = slice(h * DK, (h + 1) * DK)
            qn_ref[rows, cs] = _l2norm(acc[:, h * DK:(h + 1) * DK]) * (DK ** -0.5)
            kn_ref[rows, cs] = _l2norm(acc[:, QK_DIM + h * DK:QK_DIM + (h + 1) * DK])
        vv_ref[rows, :] = acc[:, 2 * QK_DIM:]

    bn_ref[...], gn_ref[...] = _gate_params(sm, gp_ref)

    def chunk_slices(c):
        rows = pl.ds(pl.multiple_of(c * CHUNK, CHUNK), CHUNK)
        rows2 = pl.ds(pl.multiple_of(c * 2 * CHUNK, 2 * CHUNK), 2 * CHUNK)
        tile8 = pl.ds(pl.multiple_of(c * SUBLANES, SUBLANES), SUBLANES)
        return rows, rows2, tile8

    def prepare_body(group, carry):
        _delta_prepare(group, qn_ref, kn_ref, vv_ref, bn_ref, gn_ref, wq_ref, u_ref, kd_ref, qk_ref, el_ref)
        return carry

    lax.fori_loop(0, tm // (CHUNK * PREPARE_GROUP), prepare_body, 0)

    def scan_body(c, carry):
        rows, rows2, tile8 = chunk_slices(c)
        cc = _conv_block(uext_ref, cw_ref, pl.multiple_of(c * CHUNK, CHUNK)) + cb_ref[...]
        mix_ref[rows, 0:CONV_CH] = _silu(_layer_norm(cc, cg_ref[...], cbb_ref[...]))
        o = _delta_scan_chunk(rows, rows2, tile8, wq_ref, u_ref, kd_ref, qk_ref, el_ref, s_ref)
        for h in range(HEADS):
            zc = slice(2 * CONV_CH + QKV_DIM + h * DV, 2 * CONV_CH + QKV_DIM + (h + 1) * DV)
            mix_ref[rows, CONV_CH + h * DV:CONV_CH + (h + 1) * DV] = _gdn_out(o[h], hs_ref[rows, zc], ng_ref[...])
        return carry

    lax.fori_loop(0, tm // CHUNK, scan_body, 0)

    y = alpha * x_ref[0] + _dot(_bf(mix_ref[...]), wout_ref[...])
    y_ref[0] = _layer_norm(y, lg_ref[...], lb_ref[...])

    uext_ref[0:CONV_HALO, :] = uext_ref[tm:tm + CONV_HALO, :]
    qext_ref[0:QKV_HALO, :] = qext_ref[tm:tm + QKV_HALO, :]

    @pl.when(t == pl.num_programs(1) - 1)
    def _():
        cst_ref[0] = uext_ref[off:CONV_HALO, :]
        qst_ref[0] = qext_ref[qoff:QKV_HALO, :]
        sst_ref[0] = s_ref[...]


def _prompt_mixer(x, win, wsm, cw, cb, cg, cbb, qw, gp, ng, wout, lg, lb, *, alpha, tm):
    bsz, seq, _ = x.shape
    assert seq % tm == 0 and tm % (CHUNK * PREPARE_GROUP) == 0 and tm >= CONV_HALO
    consts = [win, wsm, cw, cb, cg, cbb, qw, gp, ng, wout, lg, lb]
    return pl.pallas_call(
        functools.partial(_prompt_mixer_kernel, alpha=alpha, tm=tm),
        grid=(bsz, seq // tm),
        in_specs=[pl.BlockSpec((1, tm, D_MODEL), lambda b, t: (b, t, 0))]
        + [_const_spec(a.shape) for a in consts],
        out_specs=[
            pl.BlockSpec((1, tm, D_MODEL), lambda b, t: (b, t, 0)),
            pl.BlockSpec((1, CONV_WIDTH - 1, CONV_CH), lambda b, t: (b, 0, 0)),
            pl.BlockSpec((1, SHORT_W - 1, QKV_DIM), lambda b, t: (b, 0, 0)),
            pl.BlockSpec((1, HEADS, DK, DV), lambda b, t: (b, 0, 0, 0)),
        ],
        out_shape=[
            jax.ShapeDtypeStruct((bsz, seq, D_MODEL), F32),
            jax.ShapeDtypeStruct((bsz, CONV_WIDTH - 1, CONV_CH), F32),
            jax.ShapeDtypeStruct((bsz, SHORT_W - 1, QKV_DIM), F32),
            jax.ShapeDtypeStruct((bsz, HEADS, DK, DV), F32),
        ],
        scratch_shapes=[
            pltpu.VMEM((tm, PROJ_MAIN), F32),
            pltpu.VMEM((CONV_HALO + tm, CONV_CH), F32),
            pltpu.VMEM((QKV_HALO + tm, QKV_DIM), F32),
            pltpu.VMEM((tm, QK_DIM), F32),
            pltpu.VMEM((tm, QK_DIM), F32),
            pltpu.VMEM((tm, V_DIM), F32),
            pltpu.VMEM((tm, LANES), F32),
            pltpu.VMEM((tm, LANES), F32),
            pltpu.VMEM((tm, CONV_CH + V_DIM), F32),
            pltpu.VMEM((2 * tm, QK_DIM), F32),
            pltpu.VMEM((tm, V_DIM), F32),
            pltpu.VMEM((tm, QK_DIM), F32),
            pltpu.VMEM((tm, QK_DIM), F32),
            pltpu.VMEM((tm // CHUNK * SUBLANES, QK_DIM), F32),
            pltpu.VMEM((HEADS, DK, DV), F32),
        ],
        compiler_params=pltpu.CompilerParams(
            dimension_semantics=("parallel", "arbitrary"), vmem_limit_bytes=VMEM_LIMIT),
        name="prompt_mixer",
    )(x, *consts)


def _sample_conv_kernel(x_ref, wa_ref, wg_ref, cst_ref, cw_ref, cpre_ref, cso_ref, *, nb, ts):
    xb = _bf(x_ref[...])
    glu_all = _dot(xb, wa_ref[...]) * jax.nn.sigmoid(_dot(xb, wg_ref[...]))
    glu = [glu_all[t * nb:(t + 1) * nb, :] for t in range(ts)]

    def conv_in(s):
        return cst_ref[s] if s < CONV_WIDTH - 1 else glu[s - (CONV_WIDTH - 1)]

    for t in range(ts):
        acc = cw_ref[0:1, :] * conv_in(t)
        for j in range(1, CONV_WIDTH):
            acc = acc + cw_ref[j:j + 1, :] * conv_in(t + j)
        cpre_ref[t * nb:(t + 1) * nb, :] = acc
    for s in range(CONV_WIDTH - 1):
        cso_ref[s] = conv_in(s + ts)


def _sample_conv(x, win, cst, cw, *, nb, ts):
    n_blk = CONV_CH // LANES
    return pl.pallas_call(
        functools.partial(_sample_conv_kernel, nb=nb, ts=ts),
        grid=(n_blk,),
        in_specs=[
            _const_spec(x.shape),
            pl.BlockSpec((D_MODEL, LANES), lambda j: (0, j)),
            pl.BlockSpec((D_MODEL, LANES), lambda j: (0, n_blk + j)),
            pl.BlockSpec((CONV_WIDTH - 1, nb, LANES), lambda j: (0, 0, j)),
            pl.BlockSpec((CONV_WIDTH, LANES), lambda j: (0, j)),
        ],
        out_specs=[
            pl.BlockSpec((ts * nb, LANES), lambda j: (0, j)),
            pl.BlockSpec((CONV_WIDTH - 1, nb, LANES), lambda j: (0, 0, j)),
        ],
        out_shape=[
            jax.ShapeDtypeStruct((ts * nb, CONV_CH), F32),
            jax.ShapeDtypeStruct(cst.shape, F32),
        ],
        compiler_params=pltpu.CompilerParams(
            dimension_semantics=("parallel",), vmem_limit_bytes=VMEM_LIMIT),
        name="sample_conv",
    )(x, win, win, cst, cw)


def _sample_mixer_kernel(x_ref, cpre_ref, qst_ref, s_ref, wqk_ref, wvz_ref, wsm_ref, cb_ref, cg_ref,
                         cbb_ref, qw_ref, gp_ref, ng_ref, wout_ref, lg_ref, lb_ref,
                         y_ref, qso_ref, so_ref,
                         hqk_ref, hvz_ref, mix_ref, lw_ref, u_ref, kd_ref, qk_ref, rs_ref,
                         *, alpha, nb, ts, bb):
    i = pl.program_id(0)
    n_slab = 2 * ts
    pairs = [(a, b) for a in range(ts) for b in range(a + 1)]

    def slab(t):
        return slice(t * nb, (t + 1) * nb)

    @pl.when(i == 0)
    def _():
        xb = _bf(x_ref[...])
        hqk_ref[...] = _dot(xb, wqk_ref[...])
        hvz_ref[...] = _dot(xb, wvz_ref[...])
        sm = _dot(xb, wsm_ref[...])

        cc = _layer_norm(cpre_ref[...] + cb_ref[...], cg_ref[...], cbb_ref[...])
        mix_ref[:, 0:CONV_CH] = _silu(cc)

        for s in range(SHORT_W - 1):
            t = s + ts - (SHORT_W - 1)
            qso_ref[s, :, 0:2 * QK_DIM] = hqk_ref[slab(t), :]
            qso_ref[s, :, 2 * QK_DIM:QKV_DIM] = hvz_ref[slab(t), 0:V_DIM]

        def short_conv(col, src_ref, src_col):
            outs = []
            for t in range(ts):
                acc = None
                for j in range(SHORT_W):
                    s = t + j
                    if s < SHORT_W - 1:
                        xin = qst_ref[s, :, col:col + LANES]
                    else:
                        xin = src_ref[slab(s - (SHORT_W - 1)), src_col:src_col + LANES]
                    term = qw_ref[j:j + 1, col:col + LANES] * xin
                    acc = term if acc is None else acc + term
                outs.append(_silu(acc))
            return outs

        beta_all, g_all = _gate_params(sm, gp_ref)
        zeros = jnp.zeros((nb, LANES), F32)
        for h in range(HEADS):
            cs = slice(h * DK, (h + 1) * DK)
            q = [_l2norm(a) * (DK ** -0.5) for a in short_conv(h * DK, hqk_ref, h * DK)]
            k = [_l2norm(a) for a in short_conv(QK_DIM + h * DK, hqk_ref, QK_DIM + h * DK)]
            v = short_conv(2 * QK_DIM + h * DV, hvz_ref, h * DV)
            beta = [jnp.broadcast_to(beta_all[slab(t), h:h + 1], (nb, LANES)) for t in range(ts)]
            gcum = []
            for t in range(ts):
                gt = jnp.broadcast_to(g_all[slab(t), HEADS + h:HEADS + h + 1], (nb, LANES))
                gcum.append(gt if t == 0 else gcum[-1] + gt)
            us, ws = [], []
            for a in range(ts):
                ua = v[a] * beta[a]
                wa = k[a] * beta[a] * jnp.exp(gcum[a])
                for b in range(a):
                    a_ab = beta[a] * jnp.sum(k[a] * k[b], axis=-1, keepdims=True) * jnp.exp(gcum[a] - gcum[b])
                    ua = ua - a_ab * us[b]
                    wa = wa - a_ab * ws[b]
                us.append(ua)
                ws.append(wa)
            for t in range(ts):
                lw_ref[h, slab(t), :] = ws[t]
                lw_ref[h, slab(ts + t), :] = q[t] * jnp.exp(gcum[t])
                u_ref[h, slab(t), :] = us[t]
                u_ref[h, slab(ts + t), :] = zeros
                kd_ref[h, slab(t), :] = k[t] * jnp.exp(gcum[ts - 1] - gcum[t])
                kd_ref[h, slab(ts + t), :] = jnp.exp(gcum[ts - 1]) if t == 0 else zeros
            for p, (a, b) in enumerate(pairs):
                qk_ref[slab(p), cs] = jnp.sum(q[a] * k[b], axis=-1, keepdims=True) * jnp.exp(gcum[a] - gcum[b])

    def seq_body(grp, carry):
        keep = lax.broadcasted_iota(jnp.int32, (n_slab, LANES), 0) < ts
        items = [(bl, h) for bl in range(SEQ_GROUP) for h in range(HEADS)]
        local = {bl: grp * SEQ_GROUP + bl for bl in range(SEQ_GROUP)}
        rows = {bl: pl.ds(i * bb + local[bl], n_slab, stride=nb) for bl in range(SEQ_GROUP)}
        s = {(bl, h): s_ref[local[bl], h] for bl, h in items}
        lw = {(bl, h): lw_ref[h, rows[bl], :] for bl, h in items}
        uu = {(bl, h): u_ref[h, rows[bl], :] for bl, h in items}
        kd = {(bl, h): kd_ref[h, rows[bl], :] for bl, h in items}
        r = {it: _dot(_bf(lw[it]), _bf(s[it])) for it in items}
        v_new = {it: jnp.where(keep, uu[it] - r[it], 0.0) for it in items}
        s_new = {it: s[it] * kd[it][ts:ts + 1, :] + _dot_tn(_bf(kd[it]), _bf(v_new[it])) for it in items}
        for bl, h in items:
            rs_ref[h, rows[bl], :] = r[bl, h]
            so_ref[local[bl], h] = s_new[bl, h]
        return carry

    lax.fori_loop(0, bb // SEQ_GROUP, seq_body, 0)

    @pl.when(i == pl.num_programs(0) - 1)
    def _():
        for h in range(HEADS):
            cs = slice(h * DK, (h + 1) * DK)
            v_new = [u_ref[h, slab(t), :] - rs_ref[h, slab(t), :] for t in range(ts)]
            for a in range(ts):
                o = rs_ref[h, slab(ts + a), :]
                for b in range(a + 1):
                    o = o + qk_ref[slab(pairs.index((a, b))), cs] * v_new[b]
                zc = slice(V_DIM + h * DV, V_DIM + (h + 1) * DV)
                mix_ref[slab(a), CONV_CH + h * DV:CONV_CH + (h + 1) * DV] = _gdn_out(
                    o, hvz_ref[slab(a), zc], ng_ref[...])
        y = alpha * x_ref[...] + _dot(_bf(mix_ref[...]), wout_ref[...])
        y_ref[...] = _layer_norm(y, lg_ref[...], lb_ref[...])


def _sample_mixer(x, cst, qst, s0, win, wsm, cw, cb, cg, cbb, qw, gp, ng, wout, lg, lb, *, alpha, bb):
    nb = s0.shape[0]
    ts = x.shape[0] // nb
    assert nb % bb == 0 and bb % SEQ_GROUP == 0 and nb % SUBLANES == 0 and ts >= SHORT_W - 1
    cpre, cso = _sample_conv(x, win, cst, cw, nb=nb, ts=ts)
    consts = [wsm, cb, cg, cbb, qw, gp, ng, wout, lg, lb]
    n_pairs = ts * (ts + 1) // 2
    qk_blk = 2 * CONV_CH // (2 * QK_DIM)
    y, qso, so = pl.pallas_call(
        functools.partial(_sample_mixer_kernel, alpha=alpha, nb=nb, ts=ts, bb=bb),
        grid=(nb // bb,),
        in_specs=[
            _const_spec(x.shape), _const_spec(cpre.shape), _const_spec(qst.shape),
            pl.BlockSpec((bb, HEADS, DK, DV), lambda i: (i, 0, 0, 0)),
            pl.BlockSpec((D_MODEL, 2 * QK_DIM), lambda i: (0, qk_blk), pipeline_mode=pl.Buffered(1)),
            pl.BlockSpec((D_MODEL, 2 * V_DIM), lambda i: (0, qk_blk + 1), pipeline_mode=pl.Buffered(1)),
        ] + [_const_spec(a.shape) for a in consts],
        out_specs=[
            pl.BlockSpec(x.shape, lambda i: (0, 0)),
            pl.BlockSpec(qst.shape, lambda i: (0, 0, 0)),
            pl.BlockSpec((bb, HEADS, DK, DV), lambda i: (i, 0, 0, 0)),
        ],
        out_shape=[
            jax.ShapeDtypeStruct(x.shape, F32),
            jax.ShapeDtypeStruct(qst.shape, F32),
            jax.ShapeDtypeStruct(s0.shape, F32),
        ],
        scratch_shapes=[
            pltpu.VMEM((ts * nb, 2 * QK_DIM), F32),
            pltpu.VMEM((ts * nb, 2 * V_DIM), F32),
            pltpu.VMEM((ts * nb, CONV_CH + V_DIM), F32),
            pltpu.VMEM((HEADS, 2 * ts * nb, DK), F32),
            pltpu.VMEM((HEADS, 2 * ts * nb, DV), F32),
            pltpu.VMEM((HEADS, 2 * ts * nb, DK), F32),
            pltpu.VMEM((n_pairs * nb, QK_DIM), F32),
            pltpu.VMEM((HEADS, 2 * ts * nb, DV), F32),
        ],
        compiler_params=pltpu.CompilerParams(
            dimension_semantics=("arbitrary",), vmem_limit_bytes=VMEM_LIMIT),
        name="sample_mixer",
    )(x, cpre, qst, s0, win, win, *consts)
    return y, cso, qso, so


def _row(v):
    return v.reshape(1, -1).astype(F32)


def _layer_params(l, ffn1_w_up, ffn1_w_down, ln1_g, ln1_b, w_in, conv_w, conv_b, conv_ln_g, conv_ln_b,
                  qkv_conv_w, a_log, dt_bias, gdn_norm_g, w_out, ln2_g, ln2_b, ffn2_w_up, ffn2_w_down,
                  ln3_g, ln3_b):
    wsm = jnp.zeros((D_MODEL, LANES), BF16).at[:, 0:2 * HEADS].set(_bf(w_in[l][:, PROJ_MAIN:]))
    gp = jnp.zeros((2, LANES), F32)
    gp = gp.at[0, HEADS:2 * HEADS].set(a_log[l]).at[1, HEADS:2 * HEADS].set(dt_bias[l])
    ffn1 = (_bf(ffn1_w_up[l]), _bf(ffn1_w_down[l]), _row(ln1_g[l]), _row(ln1_b[l]))
    ffn2 = (_bf(ffn2_w_up[l]), _bf(ffn2_w_down[l]), _row(ln3_g[l]), _row(ln3_b[l]))
    mixer = (_bf(w_in[l][:, :PROJ_MAIN]), wsm, conv_w[l], _row(conv_b[l]), _row(conv_ln_g[l]),
             _row(conv_ln_b[l]), qkv_conv_w[l], gp, _row(gdn_norm_g[l]), _bf(w_out[l]),
             _row(ln2_g[l]), _row(ln2_b[l]))
    return ffn1, mixer, ffn2


def kernel(x_prompt, x_sample, state_conv, state_qkv_conv, state_recurrent, ffn1_w_up, ffn1_w_down, ln1_g, ln1_b, w_in, conv_w, conv_b, conv_ln_g, conv_ln_b, qkv_conv_w, a_log, dt_bias, gdn_norm_g, w_out, ln2_g, ln2_b, ffn2_w_up, ffn2_w_down, ln3_g, ln3_b):
    depth = w_in.shape[0]
    alpha = (2.0 * depth) ** 0.25
    bsz, seq, _ = x_prompt.shape
    nb, ts, _ = x_sample.shape
    tm_p = min(512, seq)
    yp = x_prompt
    ys = jnp.swapaxes(x_sample, 0, 1).reshape(ts * nb, D_MODEL)
    outs = [[] for _ in range(6)]
    for l in range(depth):
        ffn1, mixer, ffn2 = _layer_params(
            l, ffn1_w_up, ffn1_w_down, ln1_g, ln1_b, w_in, conv_w, conv_b, conv_ln_g, conv_ln_b,
            qkv_conv_w, a_log, dt_bias, gdn_norm_g, w_out, ln2_g, ln2_b, ffn2_w_up, ffn2_w_down,
            ln3_g, ln3_b)
        x1 = _ffn_ln(yp.reshape(bsz * seq, D_MODEL), *ffn1, alpha=alpha, tm=tm_p)
        x2, c1, q1, s1 = _prompt_mixer(x1.reshape(bsz, seq, D_MODEL), *mixer, alpha=alpha, tm=tm_p)
        yp = _ffn_ln(x2.reshape(bsz * seq, D_MODEL), *ffn2, alpha=alpha, tm=tm_p).reshape(bsz, seq, D_MODEL)
        z1 = _ffn_ln(ys, *ffn1, alpha=alpha, tm=ts * nb)
        z2, c2, q2, s2 = _sample_mixer(
            z1, jnp.swapaxes(state_conv[l], 0, 1), jnp.swapaxes(state_qkv_conv[l], 0, 1),
            state_recurrent[l], *mixer, alpha=alpha, bb=min(8, nb))
        ys = _ffn_ln(z2, *ffn2, alpha=alpha, tm=ts * nb)
        for lst, val in zip(outs, (c1, q1, s1, jnp.swapaxes(c2, 0, 1), jnp.swapaxes(q2, 0, 1), s2)):
            lst.append(val)
    ys = jnp.swapaxes(ys.reshape(ts, nb, D_MODEL), 0, 1)
    return (yp, ys) + tuple(jnp.stack(o) for o in outs)
```
